```python
import math
import jax, jax.numpy as jnp
from jax import lax
import numpy as np

D_MODEL = 2048
BATCH = 8
SEQ = 4096
DEPTH = 4

GRID_W = 64
CTX_LEN = 256
EPS = 1e-6
N_EVEN = (DEPTH + 1) // 2
N_ODD = DEPTH // 2
ADA_CHUNKS = 6

MLSTM_DIM = D_MODEL // 2
MLSTM_HEADS = 4
MLSTM_HEAD_DIM = MLSTM_DIM // MLSTM_HEADS
MLSTM_CHUNK = 64
QK_CONV = 3
S5_DIM = D_MODEL - MLSTM_DIM
S5_GROUP = 16
S5_GROUPS = S5_DIM // S5_GROUP
S5_STATE = 64
IN_COLS = 4 * MLSTM_DIM + 4 * MLSTM_HEADS + S5_DIM

DIFF_HEADS = 8
DIFF_HEAD_DIM = D_MODEL // (2 * DIFF_HEADS)
Q_BLOCK = 128
ROPE_BASE = 10000.0

N_EXPERTS = 16
EXPERT_FF = D_MODEL // 2
CAPACITY_FACTOR = 2

kernel_name = 'hybrid_mlstm_s5_diffattn_ec_dit'


def rmsnorm(x, g):
    xf = x.astype(jnp.float32)
    y = xf * lax.rsqrt(jnp.mean(xf * xf, axis=-1, keepdims=True) + EPS)
    return (y * g.astype(jnp.float32)).astype(x.dtype)


def centred_dwconv(x, w):
    k, ch = w.shape
    return lax.conv_general_dilated(x, w[:, None, :].astype(x.dtype), window_strides=(1,),
                                    padding=[(k // 2, k // 2)], dimension_numbers=('NWC', 'WIO', 'NWC'),
                                    feature_group_count=ch)


def axial_rope_tables(n):
    rows = n // GRID_W
    row = jnp.repeat(jnp.arange(rows, dtype=jnp.float32), GRID_W)
    col = jnp.tile(jnp.arange(GRID_W, dtype=jnp.float32), rows)
    axis_dim = DIFF_HEAD_DIM // 2
    inv_freq = ROPE_BASE ** (-jnp.arange(0, axis_dim, 2, dtype=jnp.float32) / axis_dim)
    ar = row[:, None] * inv_freq[None]
    ac = col[:, None] * inv_freq[None]
    ang = jnp.concatenate([ar, ar, ac, ac], axis=-1)
    return jnp.cos(ang), jnp.sin(ang)


def apply_rope(x, cos, sin):
    x1, x2, x3, x4 = jnp.split(x, 4, axis=-1)
    rot = jnp.concatenate([-x2, x1, -x4, x3], axis=-1)
    return (x * cos + rot * sin).astype(x.dtype)


def mlstm_scan(q, k, v, ig, lf, state):
    bsz, nh, seq_len, dh = q.shape
    nc = seq_len // MLSTM_CHUNK

    def to_chunks(a):
        return jnp.moveaxis(a.reshape(a.shape[:2] + (nc, MLSTM_CHUNK) + a.shape[3:]), 2, 0)

    xs = tuple(to_chunks(a) for a in (q, k, v, ig, lf))
    mask = jnp.tril(jnp.ones((MLSTM_CHUNK, MLSTM_CHUNK), dtype=bool))

    def step(carry, inp):
        c_mat, n_vec, m = carry
        qc, kc, vc, ic, fc = inp
        b = jnp.cumsum(fc, axis=-1)
        log_d = jnp.where(mask, b[..., :, None] - b[..., None, :] + ic[..., None, :], -jnp.inf)
        inter = b + m[..., None]
        m_t = jnp.maximum(jnp.max(log_d, axis=-1), inter)
        d_mat = jnp.exp(log_d - m_t[..., None])
        w_inter = jnp.exp(inter - m_t)
        s = jnp.einsum('bhtd,bhsd->bhts', qc, kc) * d_mat
        num = jnp.einsum('bhts,bhse->bhte', s, vc) + w_inter[..., None] * jnp.einsum('bhed,bhtd->bhte', c_mat, qc)
        den = jnp.sum(s, axis=-1) + w_inter * jnp.einsum('bhd,bhtd->bht', n_vec, qc)
        h = num / jnp.maximum(jnp.abs(den), jnp.exp(-m_t))[..., None]
        b_last = b[..., -1]
        w_s = b_last[..., None] - b + ic
        m_new = jnp.maximum(b_last + m, jnp.max(w_s, axis=-1))
        decay = jnp.exp(b_last + m - m_new)
        ws = jnp.exp(w_s - m_new[..., None])
        c_new = decay[..., None, None] * c_mat + jnp.einsum('bhs,bhse,bhsd->bhed', ws, vc, kc)
        n_new = decay[..., None] * n_vec + jnp.einsum('bhs,bhsd->bhd', ws, kc)
        return (c_new, n_new, m_new), h

    state, hs = lax.scan(step, state, xs)
    h = jnp.moveaxis(hs, 0, 2).reshape(bsz, nh, seq_len, dh)
    return h, state


def mlstm_bidir(qkv, gates_f, gates_b, state_f, state_b):
    flip = lambda t: jnp.flip(t, axis=2)
    h_f, s_f = mlstm_scan(*qkv, *gates_f, state_f)
    h_b, s_b = mlstm_scan(*[flip(t) for t in qkv], *[flip(t) for t in gates_b], state_b)
    return h_f + flip(h_b), s_f, s_b


def _ssm_combine(e1, e2):
    a1, b1 = e1
    a2, b2 = e2
    return a1 * a2, a2 * b1 + b2


def s5_direction(u, lam_bar, b_bar, c_mat, x0):
    bu = jnp.einsum('blgc,gpc->lbgp', u.astype(jnp.complex64), b_bar)
    a = jnp.broadcast_to(lam_bar[None, None], (u.shape[1], 1) + lam_bar.shape)
    powers, xs = lax.associative_scan(_ssm_combine, (a, bu), axis=0)
    xs = xs + powers * x0[None]
    y = jnp.real(jnp.einsum('lbgp,gcp->blgc', xs, c_mat))
    return y, xs[-1]


def s5_bidir(u, lam_bar, b_bar, c_mat, x_f, x_b):
    y_f, s_f = s5_direction(u, lam_bar[0], b_bar[0], c_mat, x_f)
    y_b, s_b = s5_direction(jnp.flip(u, axis=1), lam_bar[1], b_bar[1], c_mat, x_b)
    return y_f + jnp.flip(y_b, axis=1), s_f, s_b


def hybrid_mixer(h_ctx, h_lat, w_in, b_gates, conv_w, head_norm, lam_re, lam_im, log_step,
                 b_re, b_im, c_re, c_im, d_skip, w_glu, w_out, need_ctx):
    f32 = jnp.float32
    m_dim, nh, dh = MLSTM_DIM, MLSTM_HEADS, MLSTM_HEAD_DIM

    def to_heads(t):
        bsz, seq_len, _ = t.shape
        return t.astype(f32).reshape(bsz, seq_len, nh, dh).transpose(0, 2, 1, 3)

    def prep(h):
        z = h @ w_in
        qk, v, o, g, u = jnp.split(z, [2 * m_dim, 3 * m_dim, 4 * m_dim, 4 * m_dim + 4 * nh], axis=-1)
        q, k = jnp.split(jax.nn.silu(centred_dwconv(qk, conv_w)), 2, axis=-1)
        g = g.astype(f32) + b_gates.astype(f32)
        i_f, f_f, i_b, f_b = (jnp.swapaxes(t, 1, 2) for t in jnp.split(g, 4, axis=-1))
        qkv = (to_heads(q), to_heads(k) * dh ** -0.5, to_heads(v))
        u_g = u.astype(f32).reshape(u.shape[0], u.shape[1], S5_GROUPS, S5_GROUP)
        return qkv, (i_f, jax.nn.log_sigmoid(f_f)), (i_b, jax.nn.log_sigmoid(f_b)), o, u, u_g

    lam = lax.complex(lam_re.astype(f32), lam_im.astype(f32))
    step = jnp.exp(log_step.astype(f32))[..., None]
    lam_bar = jnp.exp(lam * step)
    b_mat = lax.complex(b_re.astype(f32), b_im.astype(f32))
    b_bar = ((lam_bar - 1.0) / lam)[..., None] * b_mat[None]
    c_mat = lax.complex(c_re.astype(f32), c_im.astype(f32))

    qkv_c, gf_c, gb_c, o_c, u_c, ug_c = prep(h_ctx)
    qkv_l, gf_l, gb_l, o_l, u_l, ug_l = prep(h_lat)
    bsz = h_lat.shape[0]
    zero_m = (jnp.zeros((bsz, nh, dh, dh), f32), jnp.zeros((bsz, nh, dh), f32), jnp.zeros((bsz, nh), f32))
    zero_s = jnp.zeros((bsz, S5_GROUPS, S5_STATE), jnp.complex64)
    hm_c, mst_f, mst_b = mlstm_bidir(qkv_c, gf_c, gb_c, zero_m, zero_m)
    hm_l, _, _ = mlstm_bidir(qkv_l, gf_l, gb_l, mst_f, mst_b)
    ys_c, sst_f, sst_b = s5_bidir(ug_c, lam_bar, b_bar, c_mat, zero_s, zero_s)
    ys_l, _, _ = s5_bidir(ug_l, lam_bar, b_bar, c_mat, sst_f, sst_b)

    def finish(hm, o, ys, u, dtype):
        bsz_, seq_len = u.shape[:2]
        hm = rmsnorm(hm, head_norm.reshape(nh, 1, dh))
        hm = hm.transpose(0, 2, 1, 3).reshape(bsz_, seq_len, m_dim) * jax.nn.sigmoid(o.astype(f32))
        y = ys.reshape(bsz_, seq_len, S5_DIM) + d_skip.astype(f32) * u.astype(f32)
        y = jax.nn.gelu(y)
        y = y * jax.nn.sigmoid(y @ w_glu.astype(f32))
        return jnp.concatenate([hm, y], axis=-1).astype(dtype) @ w_out

    out_l = finish(hm_l, o_l, ys_l, u_l, h_lat.dtype)
    out_c = finish(hm_c, o_c, ys_c, u_c, h_ctx.dtype) if need_ctx else None
    return out_c, out_l


def diff_softmax(q, k, v, lam):
    s = jnp.einsum('ibhtd,ibhsd->ibhts', q, k).astype(jnp.float32) * DIFF_HEAD_DIM ** -0.5
    a = jax.nn.softmax(s, axis=-1)
    w = a[0] - lam * a[1]
    return jnp.einsum('bhts,bhse->bhte', w.astype(v.dtype), v)


def diff_attention(h_ctx, h_lat, w_qkv, lam_vecs, head_norm, w_o, lam_init, cos, sin, need_ctx):
    nh, d = DIFF_HEADS, DIFF_HEAD_DIM

    def proj(h):
        bsz, seq_len, _ = h.shape
        q, k, v = jnp.split(h @ w_qkv, 3, axis=-1)
        q = q.reshape(bsz, seq_len, nh, 2, d).transpose(3, 0, 2, 1, 4)
        k = k.reshape(bsz, seq_len, nh, 2, d).transpose(3, 0, 2, 1, 4)
        v = v.reshape(bsz, seq_len, nh, 2 * d).transpose(0, 2, 1, 3)
        return q, k, v

    lv = lam_vecs.astype(jnp.float32)
    lam = jnp.exp(jnp.sum(lv[0] * lv[1])) - jnp.exp(jnp.sum(lv[2] * lv[3])) + lam_init
    qc, kc, vc = proj(h_ctx)
    ql, kl, vl = proj(h_lat)
    ql = apply_rope(ql, cos, sin)
    kl = apply_rope(kl, cos, sin)
    k_all = jnp.concatenate([kc, kl], axis=3)
    v_all = jnp.concatenate([vc, vl], axis=2)
    bsz, seq_len = h_lat.shape[:2]
    nb = seq_len // Q_BLOCK
    qb = jnp.moveaxis(ql.reshape(2, bsz, nh, nb, Q_BLOCK, d), 3, 0)
    out = lax.map(lambda qblk: diff_softmax(qblk, k_all, v_all, lam), qb)
    out_l = jnp.moveaxis(out, 0, 2).reshape(bsz, nh, seq_len, 2 * d)

    def finish(o):
        bsz_, _, seq_len_, _ = o.shape
        o = rmsnorm(o, head_norm.reshape(nh, 1, 2 * d)) * (1.0 - lam_init)
        return o.transpose(0, 2, 1, 3).reshape(bsz_, seq_len_, D_MODEL).astype(h_lat.dtype) @ w_o

    out_c = finish(diff_softmax(qc, kc, vc, lam)) if need_ctx else None
    return out_c, finish(out_l)


def expert_choice_ffn(h, w_router, w_gate, w_up, w_down):
    n, dm = h.shape[1], h.shape[2]
    cap = CAPACITY_FACTOR * n // N_EXPERTS
    aff = jax.nn.softmax((h @ w_router).astype(jnp.float32), axis=-1)
    gate, idx = lax.top_k(jnp.swapaxes(aff, 1, 2), cap)
    xs = jax.vmap(lambda hb, ib: hb[ib])(h, idx)
    hid = jax.nn.silu(jnp.einsum('becd,edf->becf', xs, w_gate)) * jnp.einsum('becd,edf->becf', xs, w_up)
    out = jnp.einsum('becf,efd->becd', hid, w_down) * gate[..., None].astype(h.dtype)
    return jax.vmap(lambda ib, ob: jnp.zeros((n, dm), ob.dtype).at[ib.reshape(-1)].add(ob.reshape(-1, dm)))(idx, out)


def setup_inputs(seed: int = 0) -> dict:
    key = jax.random.key(seed)
    ks = iter(jax.random.split(key, 40))
    f32 = jnp.float32
    D = D_MODEL

    def nrm(shape, scale):
        return jax.random.normal(next(ks), shape, f32) * scale

    inp = {}
    inp['x'] = nrm((BATCH, SEQ, D), 1.0)
    inp['c'] = nrm((BATCH, D), 1.0)
    inp['ctx'] = nrm((BATCH, CTX_LEN, D), 1.0)
    inp['c_ctx'] = nrm((D,), 1.0)
    inp['ada_w'] = nrm((DEPTH, D, ADA_CHUNKS * D), 0.5 * D ** -0.5)
    inp['ada_b'] = nrm((DEPTH, ADA_CHUNKS * D), 0.02)
    inp['norm_mix'] = 1.0 + nrm((DEPTH, D), 0.02)
    inp['norm_ffn'] = 1.0 + nrm((DEPTH, D), 0.02)
    inp['norm_final'] = 1.0 + nrm((D,), 0.02)
    inp['hy_w_in'] = nrm((N_EVEN, D, IN_COLS), D ** -0.5)
    i_bias = nrm((N_EVEN, 2, 1, MLSTM_HEADS), 0.1)
    f_bias = jnp.linspace(3.0, 6.0, MLSTM_HEADS, dtype=f32) + nrm((N_EVEN, 2, 1, MLSTM_HEADS), 0.1)
    inp['hy_b_gates'] = jnp.concatenate([i_bias, f_bias], axis=2).reshape(N_EVEN, 4 * MLSTM_HEADS)
    inp['hy_conv'] = nrm((N_EVEN, QK_CONV, 2 * MLSTM_DIM), QK_CONV ** -0.5)
    inp['hy_head_norm'] = 1.0 + nrm((N_EVEN, MLSTM_DIM), 0.02)
    inp['s5_lam_re'] = -0.5 + nrm((N_EVEN, 2, S5_GROUPS, S5_STATE), 0.01)
    inp['s5_lam_im'] = jnp.pi * jnp.arange(S5_STATE, dtype=f32) + nrm((N_EVEN, 2, S5_GROUPS, S5_STATE), 0.01)
    inp['s5_log_step'] = jax.random.uniform(next(ks), (N_EVEN, 2, S5_GROUPS), f32, math.log(1e-3), math.log(1e-1))
    inp['s5_b_re'] = nrm((N_EVEN, S5_GROUPS, S5_STATE, S5_GROUP), (2 * S5_GROUP) ** -0.5)
    inp['s5_b_im'] = nrm((N_EVEN, S5_GROUPS, S5_STATE, S5_GROUP), (2 * S5_GROUP) ** -0.5)
    inp['s5_c_re'] = nrm((N_EVEN, S5_GROUPS, S5_GROUP, S5_STATE), S5_STATE ** -0.5)
    inp['s5_c_im'] = nrm((N_EVEN, S5_GROUPS, S5_GROUP, S5_STATE), S5_STATE ** -0.5)
    inp['s5_d'] = nrm((N_EVEN, S5_DIM), 0.5)
    inp['s5_w_glu'] = nrm((N_EVEN, S5_DIM, S5_DIM), S5_DIM ** -0.5)
    inp['hy_w_out'] = nrm((N_EVEN, D, D), D ** -0.5)
    inp['da_w_qkv'] = nrm((N_ODD, D, 3 * D), D ** -0.5)
    inp['da_lambda'] = nrm((N_ODD, 4, DIFF_HEAD_DIM), 0.1)
    inp['da_head_norm'] = 1.0 + nrm((N_ODD, D), 0.02)
    inp['da_w_o'] = nrm((N_ODD, D, D), D ** -0.5)
    inp['ec_router'] = nrm((DEPTH, D, N_EXPERTS), D ** -0.5)
    inp['ec_w_gate'] = nrm((DEPTH, N_EXPERTS, D, EXPERT_FF), D ** -0.5)
    inp['ec_w_up'] = nrm((DEPTH, N_EXPERTS, D, EXPERT_FF), D ** -0.5)
    inp['ec_w_down'] = nrm((DEPTH, N_EXPERTS, EXPERT_FF, D), EXPERT_FF ** -0.5)
    return inp


def reference(x, c, ctx, c_ctx, ada_w, ada_b, norm_mix, norm_ffn, norm_final, hy_w_in, hy_b_gates, hy_conv,
              hy_head_norm, s5_lam_re, s5_lam_im, s5_log_step, s5_b_re, s5_b_im, s5_c_re, s5_c_im, s5_d, s5_w_glu,
              hy_w_out, da_w_qkv, da_lambda, da_head_norm, da_w_o, ec_router, ec_w_gate, ec_w_up, ec_w_down):
    cos, sin = axial_rope_tables(x.shape[1])
    sc_lat = jax.nn.silu(c)
    sc_ctx = jax.nn.silu(c_ctx)
    for layer in range(DEPTH):
        need_ctx = layer < DEPTH - 1
        j = layer // 2
        mod_l = jnp.split((sc_lat @ ada_w[layer] + ada_b[layer])[:, None, :], ADA_CHUNKS, axis=-1)
        mod_c = jnp.split((sc_ctx @ ada_w[layer] + ada_b[layer])[None, None, :], ADA_CHUNKS, axis=-1)
        h_l = rmsnorm(x, norm_mix[layer]) * (1.0 + mod_l[1]) + mod_l[0]
        h_c = rmsnorm(ctx, norm_mix[layer]) * (1.0 + mod_c[1]) + mod_c[0]
        if layer % 2 == 0:
            mix_c, mix_l = hybrid_mixer(h_c, h_l, hy_w_in[j], hy_b_gates[j], hy_conv[j], hy_head_norm[j],
                                        s5_lam_re[j], s5_lam_im[j], s5_log_step[j], s5_b_re[j], s5_b_im[j],
                                        s5_c_re[j], s5_c_im[j], s5_d[j], s5_w_glu[j], hy_w_out[j], need_ctx)
        else:
            lam_init = 0.8 - 0.6 * math.exp(-0.3 * layer)
            mix_c, mix_l = diff_attention(h_c, h_l, da_w_qkv[j], da_lambda[j], da_head_norm[j], da_w_o[j],
                                          lam_init, cos, sin, need_ctx)
        x = x + mod_l[2] * mix_l
        h_l = rmsnorm(x, norm_ffn[layer]) * (1.0 + mod_l[4]) + mod_l[3]
        x = x + mod_l[5] * expert_choice_ffn(h_l, ec_router[layer], ec_w_gate[layer], ec_w_up[layer], ec_w_down[layer])
        if need_ctx:
            ctx = ctx + mod_c[2] * mix_c
            h_c = rmsnorm(ctx, norm_ffn[layer]) * (1.0 + mod_c[4]) + mod_c[3]
            ctx = ctx + mod_c[5] * expert_choice_ffn(h_c, ec_router[layer], ec_w_gate[layer], ec_w_up[layer], ec_w_down[layer])
    return rmsnorm(x, norm_final)
```

```python
import functools
import math

import jax
import jax.numpy as jnp
from jax import lax
from jax.experimental import pallas as pl
from jax.experimental.pallas import tpu as pltpu

F32 = jnp.float32
BF16 = jnp.bfloat16

GRID_W = 64
EPS = 1e-6
ADA_CHUNKS = 6
MLSTM_HEADS = 4
QK_CONV = 3
S5_GROUP = 16
DIFF_HEADS = 8
ROPE_BASE = 10000.0
CAPACITY_FACTOR = 2

MLSTM_TILE = 256
S5_TILE = 16
VMEM_LIMIT_BYTES = 56 * 1024 * 1024
ROW_TILE = 1024
COL_TILE = 1024
ATTN_Q_TILE = 256
LANES = 128


def _cparams(*sem):
    return pltpu.CompilerParams(dimension_semantics=sem, vmem_limit_bytes=VMEM_LIMIT_BYTES)


def _tile(n, pref):
    t = min(n, pref)
    while n % t:
        t //= 2
    return t


def _ada_kernel(c_ref, w_ref, b_ref, o_ref):
    cv = c_ref[...]
    a = (cv * jax.nn.sigmoid(cv)).astype(BF16)
    o_ref[0] = jnp.dot(a, w_ref[0].astype(BF16), preferred_element_type=F32) + b_ref[0]


def ada_modulation(cond, ada_w, ada_b):
    depth, d, n = ada_w.shape
    rows = cond.shape[0]
    tn = _tile(n, COL_TILE)
    return pl.pallas_call(
        _ada_kernel,
        grid=(depth, n // tn),
        in_specs=[
            pl.BlockSpec((rows, d), lambda l, j: (0, 0)),
            pl.BlockSpec((1, d, tn), lambda l, j: (l, 0, j)),
            pl.BlockSpec((1, 1, tn), lambda l, j: (l, 0, j)),
        ],
        out_specs=pl.BlockSpec((1, rows, tn), lambda l, j: (l, 0, j)),
        out_shape=jax.ShapeDtypeStruct((depth, rows, n), F32),
        compiler_params=_cparams("arbitrary", "arbitrary"),
    )(cond, ada_w, ada_b.reshape(depth, 1, n))


def _normmod_kernel(x_ref, g_ref, sc_ref, sh_ref, *o_refs):
    x = x_ref[0]
    y = x * lax.rsqrt(jnp.mean(x * x, axis=-1, keepdims=True) + EPS) * g_ref[...]
    y = y * (1.0 + sc_ref[0]) + sh_ref[0]
    for o_ref in o_refs:
        o_ref[0] = y.astype(o_ref.dtype)


def norm_modulate(x, g, scale, shift, dtypes):
    bsz, seq, d = x.shape
    tl = _tile(seq, 512)
    return pl.pallas_call(
        _normmod_kernel,
        grid=(bsz, seq // tl),
        in_specs=[
            pl.BlockSpec((1, tl, d), lambda b, i: (b, i, 0)),
            pl.BlockSpec((1, d), lambda b, i: (0, 0)),
            pl.BlockSpec((1, 1, d), lambda b, i: (b, 0, 0)),
            pl.BlockSpec((1, 1, d), lambda b, i: (b, 0, 0)),
        ],
        out_specs=[pl.BlockSpec((1, tl, d), lambda b, i: (b, i, 0)) for _ in dtypes],
        out_shape=[jax.ShapeDtypeStruct(x.shape, dt) for dt in dtypes],
        compiler_params=_cparams("arbitrary", "arbitrary"),
    )(x, g.reshape(1, d), scale, shift)


def _final_norm_kernel(x_ref, g_ref, o_ref):
    x = x_ref[0]
    o_ref[0] = x * lax.rsqrt(jnp.mean(x * x, axis=-1, keepdims=True) + EPS) * g_ref[...]


def final_norm(x, g):
    bsz, seq, d = x.shape
    tl = _tile(seq, 512)
    return pl.pallas_call(
        _final_norm_kernel,
        grid=(bsz, seq // tl),
        in_specs=[pl.BlockSpec((1, tl, d), lambda b, i: (b, i, 0)), pl.BlockSpec((1, d), lambda b, i: (0, 0))],
        out_specs=pl.BlockSpec((1, tl, d), lambda b, i: (b, i, 0)),
        out_shape=jax.ShapeDtypeStruct(x.shape, F32),
        compiler_params=_cparams("arbitrary", "arbitrary"),
    )(x, g.reshape(1, d))


def _mm_kernel(n_pairs, *refs):
    o_ref = refs[2 * n_pairs]
    acc = jnp.dot(refs[0][...], refs[1][...], preferred_element_type=F32)
    for p in range(1, n_pairs):
        acc += jnp.dot(refs[2 * p][...], refs[2 * p + 1][...], preferred_element_type=F32)
    o_ref[...] = acc.astype(o_ref.dtype)


def _mm_residual_kernel(n_pairs, *refs):
    r_ref, g_ref, o_ref = refs[2 * n_pairs : 2 * n_pairs + 3]
    acc = jnp.dot(refs[0][...], refs[1][...], preferred_element_type=F32)
    for p in range(1, n_pairs):
        acc += jnp.dot(refs[2 * p][...], refs[2 * p + 1][...], preferred_element_type=F32)
    o_ref[...] = r_ref[...] + g_ref[0] * acc


def _mm_rope_kernel(n_rope_tiles, quarter, a_ref, w_ref, cos_ref, sin_ref, o_ref):
    acc = jnp.dot(a_ref[...], w_ref[...], preferred_element_type=F32)
    j = pl.program_id(1)

    @pl.when(j >= n_rope_tiles)
    def _():
        o_ref[...] = acc.astype(o_ref.dtype)

    @pl.when(j < n_rope_tiles)
    def _():
        cos = cos_ref[...]
        sin = sin_ref[...]
        hd = cos.shape[1]
        lane = lax.broadcasted_iota(jnp.int32, (acc.shape[0], hd), 1)
        first = (lane % (2 * quarter)) < quarter
        for g in range(acc.shape[1] // hd):
            xg = acc[:, g * hd : (g + 1) * hd]
            up = pltpu.roll(xg, hd - quarter, 1)
            dn = pltpu.roll(xg, quarter, 1)
            o_ref[:, g * hd : (g + 1) * hd] = (xg * cos + jnp.where(first, up, dn) * sin).astype(o_ref.dtype)


def matmul(pairs, out_dtype, *, tm=ROW_TILE, tn=COL_TILE, residual=None, gate=None, rows_per_batch=None, rope=None):
    m = pairs[0][0].shape[0]
    n = pairs[0][1].shape[1]
    if rope is not None:
        rows_per_batch = rope[3]
    tm = _tile(m if rows_per_batch is None else rows_per_batch, tm)
    tn = _tile(n, tn)
    grid = (m // tm, n // tn)
    in_specs, args = [], []
    for a, w in pairs:
        k = a.shape[1]
        in_specs += [pl.BlockSpec((tm, k), lambda i, j: (i, 0)), pl.BlockSpec((k, tn), lambda i, j: (0, j))]
        args += [a, w]
    out_spec = pl.BlockSpec((tm, tn), lambda i, j: (i, j))
    aliases = {}
    if residual is not None:
        bpt = rows_per_batch // tm
        in_specs += [pl.BlockSpec((tm, tn), lambda i, j: (i, j)), pl.BlockSpec((1, 1, tn), lambda i, j: (i // bpt, 0, j))]
        args += [residual, gate]
        aliases = {2 * len(pairs): 0}
        body = functools.partial(_mm_residual_kernel, len(pairs))
    elif rope is not None:
        cos, sin, n_rope_cols, rpb = rope
        hd = cos.shape[1]
        bpt = rpb // tm
        in_specs += [pl.BlockSpec((tm, hd), lambda i, j: (i % bpt, 0)), pl.BlockSpec((tm, hd), lambda i, j: (i % bpt, 0))]
        args += [cos, sin]
        body = functools.partial(_mm_rope_kernel, n_rope_cols // tn, hd // 4)
    else:
        body = functools.partial(_mm_kernel, len(pairs))
    return pl.pallas_call(
        body,
        grid=grid,
        in_specs=in_specs,
        out_specs=out_spec,
        out_shape=jax.ShapeDtypeStruct((m, n), out_dtype),
        input_output_aliases=aliases,
        compiler_params=_cparams("arbitrary", "arbitrary"),
    )(*args)


def _attn_kernel(n_seg, hd, out_scale, lam_ref, q_ref, *refs):
    kv_refs = refs[: 2 * n_seg]
    g_ref, o_ref = refs[2 * n_seg], refs[2 * n_seg + 1]
    lam = lam_ref[0]
    scale = hd**-0.5
    q = q_ref[0]
    outs = []
    nt = (((1,), (1,)), ((), ()))
    for i in range(2):
        qi = q[:, i * hd : (i + 1) * hd]
        scores = [lax.dot_general(qi, kv_refs[2 * s][0][:, i * hd : (i + 1) * hd], nt, preferred_element_type=F32)
                  for s in range(n_seg)]
        m = scores[0].max(axis=-1, keepdims=True)
        for sc in scores[1:]:
            m = jnp.maximum(m, sc.max(axis=-1, keepdims=True))
        denom = 0.0
        acc = 0.0
        for s, sc in enumerate(scores):
            p = jnp.exp((sc - m) * scale)
            denom = denom + p.sum(axis=-1, keepdims=True)
            acc = acc + jnp.dot(p.astype(BF16), kv_refs[2 * s + 1][0], preferred_element_type=F32)
        outs.append(acc / denom)
    o = outs[0] - lam * outs[1]
    o = o * lax.rsqrt(jnp.mean(o * o, axis=-1, keepdims=True) + EPS) * g_ref[...] * out_scale
    o_ref[0] = o.astype(o_ref.dtype)


def diff_attention(q_src, kv_srcs, lam, head_norm, out_scale):
    bsz, lq, d3 = q_src.shape
    d = d3 // 3
    hw = d // DIFF_HEADS
    hd = hw // 2
    tq = _tile(lq, ATTN_Q_TILE)
    in_specs = [
        pl.BlockSpec(memory_space=pltpu.SMEM),
        pl.BlockSpec((1, tq, hw), lambda b, h, i: (b, i, h)),
    ]
    args = [lam.reshape(1), q_src]
    for src in kv_srcs:
        ls = src.shape[1]
        in_specs += [
            pl.BlockSpec((1, ls, hw), lambda b, h, i: (b, 0, DIFF_HEADS + h)),
            pl.BlockSpec((1, ls, hw), lambda b, h, i: (b, 0, 2 * DIFF_HEADS + h)),
        ]
        args += [src, src]
    in_specs.append(pl.BlockSpec((1, hw), lambda b, h, i: (0, h)))
    args.append(head_norm.reshape(1, d))
    return pl.pallas_call(
        functools.partial(_attn_kernel, len(kv_srcs), hd, out_scale),
        grid=(bsz, DIFF_HEADS, lq // tq),
        in_specs=in_specs,
        out_specs=pl.BlockSpec((1, tq, hw), lambda b, h, i: (b, i, h)),
        out_shape=jax.ShapeDtypeStruct((bsz, lq, d), BF16),
        compiler_params=_cparams("arbitrary", "arbitrary", "arbitrary"),
    )(*args)


def _router_kernel(h_ref, w_ref, o_ref):
    nt = (((1,), (1,)), ((), ()))
    logits = lax.dot_general(w_ref[...], h_ref[0], nt, preferred_element_type=F32)
    z = jnp.exp(logits - logits.max(axis=0, keepdims=True))
    o_ref[0] = z / z.sum(axis=0, keepdims=True)


def router_affinity(h, w_router_t):
    bsz, seq, d = h.shape
    ne = w_router_t.shape[0]
    tl = _tile(seq, 1024)
    return pl.pallas_call(
        _router_kernel,
        grid=(bsz, seq // tl),
        in_specs=[pl.BlockSpec((1, tl, d), lambda b, i: (b, i, 0)), pl.BlockSpec((ne, d), lambda b, i: (0, 0))],
        out_specs=pl.BlockSpec((1, ne, tl), lambda b, i: (b, 0, i)),
        out_shape=jax.ShapeDtypeStruct((bsz, ne, seq), F32),
        compiler_params=_cparams("arbitrary", "arbitrary"),
    )(h, w_router_t)


def _row_copy(src, src_row, dst, dst_row, sem):
    return pltpu.make_async_copy(src.at[pl.ds(src_row, 1)], dst.at[pl.ds(dst_row, 1)], sem)


def _moe_kernel(cap, idx_ref, gate_ref, mod_ref, h_hbm, x_in, wg_ref, wu_ref, wd_ref, x_out, xs_buf, xr_buf, sems):
    del x_in
    b = pl.program_id(1)

    def issue(c, carry):
        t = idx_ref[0, 0, c]
        _row_copy(h_hbm.at[b], t, xs_buf, c, sems.at[0]).start()
        _row_copy(x_out.at[b], t, xr_buf, c, sems.at[1]).start()
        return carry

    lax.fori_loop(0, cap, issue, 0)

    def wait_rows(buf, sem):
        def body(c, carry):
            _row_copy(h_hbm.at[b], 0, buf, c, sem).wait()
            return carry

        lax.fori_loop(0, cap, body, 0)

    wait_rows(xs_buf, sems.at[0])
    xs = xs_buf[...].astype(BF16)
    gt = jnp.dot(xs, wg_ref[0], preferred_element_type=F32)
    up = jnp.dot(xs, wu_ref[0], preferred_element_type=F32)
    hid = (gt * jax.nn.sigmoid(gt) * up).astype(BF16)
    out = jnp.dot(hid, wd_ref[0], preferred_element_type=F32) * gate_ref[0] * mod_ref[0]
    wait_rows(xr_buf, sems.at[1])
    xr_buf[...] += out

    def scatter(c, carry):
        t = idx_ref[0, 0, c]
        _row_copy(xr_buf, c, x_out.at[b], t, sems.at[2]).start()
        return carry

    lax.fori_loop(0, cap, scatter, 0)

    def wait_scatter(c, carry):
        _row_copy(xr_buf, c, x_out.at[b], 0, sems.at[2]).wait()
        return carry

    lax.fori_loop(0, cap, wait_scatter, 0)


def moe_apply(x, h, idx, gate, mod, w_gate, w_up, w_down):
    bsz, seq, d = x.shape
    ne, cap = idx.shape[1], idx.shape[2]
    ff = w_gate.shape[2]
    return pl.pallas_call(
        functools.partial(_moe_kernel, cap),
        grid=(ne, bsz),
        in_specs=[
            pl.BlockSpec((1, 1, cap), lambda e, b: (b * ne + e, 0, 0), memory_space=pltpu.SMEM),
            pl.BlockSpec((1, cap, 1), lambda e, b: (b * ne + e, 0, 0)),
            pl.BlockSpec((1, 1, d), lambda e, b: (b, 0, 0)),
            pl.BlockSpec(memory_space=pl.ANY),
            pl.BlockSpec(memory_space=pl.ANY),
            pl.BlockSpec((1, d, ff), lambda e, b: (e, 0, 0)),
            pl.BlockSpec((1, d, ff), lambda e, b: (e, 0, 0)),
            pl.BlockSpec((1, ff, d), lambda e, b: (e, 0, 0)),
        ],
        out_specs=pl.BlockSpec(memory_space=pl.ANY),
        out_shape=jax.ShapeDtypeStruct(x.shape, F32),
        scratch_shapes=[
            pltpu.VMEM((cap, d), F32),
            pltpu.VMEM((cap, d), F32),
            pltpu.SemaphoreType.DMA((3,)),
        ],
        input_output_aliases={4: 0},
        compiler_params=_cparams("arbitrary", "arbitrary"),
    )(idx.reshape(bsz * ne, 1, cap), gate.reshape(bsz * ne, cap, 1), mod, h, x, w_gate, w_up, w_down)


def expert_choice_ffn(x, h_bf16, h_f32, mod, w_router_t, w_gate, w_up, w_down):
    seq = x.shape[1]
    ne = w_router_t.shape[0]
    cap = CAPACITY_FACTOR * seq // ne
    aff = router_affinity(h_bf16, w_router_t)
    gate, idx = lax.top_k(aff, cap)
    return moe_apply(x, h_f32, idx.astype(jnp.int32), gate, mod, w_gate, w_up, w_down)


def _conv_kernel(z_ref, w_ref, s_ref, o_ref):
    x = z_ref[0]
    seq = x.shape[0]
    row = lax.broadcasted_iota(jnp.int32, x.shape, 0)
    prev = jnp.where(row == 0, 0.0, pltpu.roll(x, 1, 0))
    nxt = jnp.where(row == seq - 1, 0.0, pltpu.roll(x, seq - 1, 0))
    w = w_ref[...]
    y = prev * w[0:1] + x * w[1:2] + nxt * w[2:3]
    o_ref[0] = (y * jax.nn.sigmoid(y) * s_ref[...]).astype(o_ref.dtype)


def qk_conv(z, conv_w, col_scale, n_cols):
    bsz, seq = z.shape[:2]
    tc = 256
    return pl.pallas_call(
        _conv_kernel,
        grid=(bsz, n_cols // tc),
        in_specs=[
            pl.BlockSpec((1, seq, tc), lambda b, j: (b, 0, j)),
            pl.BlockSpec((QK_CONV, tc), lambda b, j: (0, j)),
            pl.BlockSpec((1, tc), lambda b, j: (0, j)),
        ],
        out_specs=pl.BlockSpec((1, seq, tc), lambda b, j: (b, 0, j)),
        out_shape=jax.ShapeDtypeStruct((bsz, seq, n_cols), BF16),
        compiler_params=_cparams("arbitrary", "arbitrary"),
    )(z, conv_w, col_scale)


def _log_sigmoid(x):
    return jnp.minimum(x, 0.0) - jnp.log(1.0 + jnp.exp(-jnp.abs(x)))


def _mlstm_kernel(nh, dh, *refs):
    dir_refs = (refs[0:6], refs[6:12])
    bias_ref, bias_t_ref, c0_ref, n0_ref, m0_ref = refs[12:17]
    h_outs = refs[17:19]
    c_out, n_out, m_out = refs[19:22]
    c_s, n_s, m_s = refs[22:25]
    kc = pl.program_id(1)
    last = pl.num_programs(1) - 1

    @pl.when(kc == 0)
    def _():
        c_s[...] = c0_ref[0]
        n_s[...] = n0_ref[0]
        m_s[...] = m0_ref[0]

    tt = dir_refs[0][0].shape[1]
    row = lax.broadcasted_iota(jnp.int32, (tt, tt), 0)
    col = lax.broadcasted_iota(jnp.int32, (tt, tt), 1)
    nt = (((1,), (1,)), ((), ()))
    hi = lax.Precision.HIGHEST
    for d in range(2):
        q_ref, k_ref, kt_ref, v_ref, g_ref, gt_ref = dir_refs[d]
        window = (col <= row) if d == 0 else (col >= row)
        tri = window.astype(F32)
        gates = g_ref[0] + bias_ref[...]
        gates_t = gt_ref[0] + bias_t_ref[...]
        b_cols = jnp.dot(tri, _log_sigmoid(gates), precision=hi, preferred_element_type=F32)
        b_rows = lax.dot_general(_log_sigmoid(gates_t), tri, nt, precision=hi, preferred_element_type=F32)
        edge = tt - 1 if d == 0 else 0
        for h in range(nh):
            ci = 2 * nh * d + h
            cf = ci + nh
            sl = slice(h * dh, (h + 1) * dh)
            st = d * nh + h
            i_col, i_row = gates[:, ci : ci + 1], gates_t[ci : ci + 1, :]
            b_col, b_row = b_cols[:, cf : cf + 1], b_rows[cf : cf + 1, :]
            m_prev = m_s[st][0:1, 0:1]
            log_d = jnp.where(window, b_col - b_row + i_row, -jnp.inf)
            inter = b_col + m_prev
            m_t = jnp.maximum(log_d.max(axis=-1, keepdims=True), inter)
            d_mat = jnp.exp(log_d - m_t)
            w_inter = jnp.exp(inter - m_t)
            qh = q_ref[0][:, sl]
            vh = v_ref[0][:, sl]
            s_mat = jnp.dot(qh, kt_ref[0][sl, :], preferred_element_type=F32) * d_mat
            c_t = c_s[st]
            n_row = n_s[st][0:1, :]
            num = jnp.dot(s_mat.astype(BF16), vh.astype(BF16), preferred_element_type=F32)
            num += w_inter * jnp.dot(qh, c_t.astype(BF16), preferred_element_type=F32)
            den = s_mat.sum(axis=-1, keepdims=True) + w_inter * (qh.astype(F32) * n_row).sum(axis=-1, keepdims=True)
            h_outs[d][0, :, sl] = num / jnp.maximum(jnp.abs(den), jnp.exp(-m_t))
            b_last = b_col[edge : edge + 1, :]
            ws_col = b_last - b_col + i_col
            m_new = jnp.maximum(b_last + m_prev, ws_col.max(axis=0, keepdims=True))
            decay = jnp.exp(b_last + m_prev - m_new)
            ws_col = jnp.exp(ws_col - m_new)
            c_s[st] = decay * c_t + jnp.dot(kt_ref[0][sl, :], (ws_col * vh).astype(BF16), preferred_element_type=F32)
            n_new = decay * n_row + (ws_col * k_ref[0][:, sl].astype(F32)).sum(axis=0, keepdims=True)
            n_s[st] = jnp.broadcast_to(n_new, n_s.shape[1:])
            m_s[st] = jnp.broadcast_to(m_new, m_s.shape[1:])

    @pl.when(kc == last)
    def _():
        c_out[0] = c_s[...]
        n_out[0] = n_s[...]
        m_out[0] = m_s[...]


def mlstm_bidir(qk, qk_t, z, v_col_block, gates, gates_t, bias, state):
    bsz, seq, m2 = qk.shape
    md = m2 // 2
    nh = MLSTM_HEADS
    dh = md // nh
    tt = _tile(seq, MLSTM_TILE)
    nc = seq // tt
    fwd = lambda b, k: k
    bwd = lambda b, k: nc - 1 - k
    in_specs, args = [], []
    for ck in (fwd, bwd):
        in_specs += [
            pl.BlockSpec((1, tt, md), lambda b, k, ck=ck: (b, ck(b, k), 0)),
            pl.BlockSpec((1, tt, md), lambda b, k, ck=ck: (b, ck(b, k), 1)),
            pl.BlockSpec((1, md, tt), lambda b, k, ck=ck: (b, 1, ck(b, k))),
            pl.BlockSpec((1, tt, md), lambda b, k, ck=ck: (b, ck(b, k), v_col_block)),
            pl.BlockSpec((1, tt, LANES), lambda b, k, ck=ck: (b, ck(b, k), 0)),
            pl.BlockSpec((1, 4 * nh, tt), lambda b, k, ck=ck: (b, 0, ck(b, k))),
        ]
        args += [qk, qk, qk_t, z, gates, gates_t]
    c0, n0, m0 = state
    ns = 2 * nh
    in_specs += [
        pl.BlockSpec((1, LANES), lambda b, k: (0, 0)),
        pl.BlockSpec((4 * nh, 1), lambda b, k: (0, 0)),
        pl.BlockSpec((1, ns, dh, dh), lambda b, k: (b, 0, 0, 0)),
        pl.BlockSpec((1, ns, 8, dh), lambda b, k: (b, 0, 0, 0)),
        pl.BlockSpec((1, ns, 8, LANES), lambda b, k: (b, 0, 0, 0)),
    ]
    bias_row = jnp.zeros((1, LANES), F32).at[0, : 4 * nh].set(bias)
    args += [bias_row, bias.reshape(4 * nh, 1), c0, n0, m0]
    out_specs = [
        pl.BlockSpec((1, tt, md), lambda b, k: (b, k, 0)),
        pl.BlockSpec((1, tt, md), lambda b, k: (b, nc - 1 - k, 0)),
        pl.BlockSpec((1, ns, dh, dh), lambda b, k: (b, 0, 0, 0)),
        pl.BlockSpec((1, ns, 8, dh), lambda b, k: (b, 0, 0, 0)),
        pl.BlockSpec((1, ns, 8, LANES), lambda b, k: (b, 0, 0, 0)),
    ]
    out_shape = [
        jax.ShapeDtypeStruct((bsz, seq, md), F32),
        jax.ShapeDtypeStruct((bsz, seq, md), F32),
        jax.ShapeDtypeStruct(c0.shape, F32),
        jax.ShapeDtypeStruct(n0.shape, F32),
        jax.ShapeDtypeStruct(m0.shape, F32),
    ]
    h_f, h_b, c1, n1, m1 = pl.pallas_call(
        functools.partial(_mlstm_kernel, nh, dh),
        grid=(bsz, nc),
        in_specs=in_specs,
        out_specs=out_specs,
        out_shape=out_shape,
        scratch_shapes=[pltpu.VMEM((ns, dh, dh), F32), pltpu.VMEM((ns, 8, dh), F32), pltpu.VMEM((ns, 8, LANES), F32)],
        compiler_params=_cparams("arbitrary", "arbitrary"),
    )(*args)
    return h_f, h_b, (c1, n1, m1)


def _mlstm_finish_kernel(nh, hf_ref, hb_ref, o_ref, g_ref, out_ref):
    hm = hf_ref[0] + hb_ref[0]
    dh = hm.shape[1] // nh
    gate = jax.nn.sigmoid(o_ref[0])
    g = g_ref[...]
    for h in range(nh):
        sl = slice(h * dh, (h + 1) * dh)
        xh = hm[:, sl]
        y = xh * lax.rsqrt(jnp.mean(xh * xh, axis=-1, keepdims=True) + EPS) * g[:, sl]
        out_ref[0, :, sl] = (y * gate[:, sl]).astype(out_ref.dtype)


def mlstm_finish(h_f, h_b, z, o_col_block, head_norm):
    bsz, seq, md = h_f.shape
    tl = _tile(seq, 512)
    return pl.pallas_call(
        functools.partial(_mlstm_finish_kernel, MLSTM_HEADS),
        grid=(bsz, seq // tl),
        in_specs=[
            pl.BlockSpec((1, tl, md), lambda b, i: (b, i, 0)),
            pl.BlockSpec((1, tl, md), lambda b, i: (b, i, 0)),
            pl.BlockSpec((1, tl, md), lambda b, i: (b, i, o_col_block)),
            pl.BlockSpec((1, md), lambda b, i: (0, 0)),
        ],
        out_specs=pl.BlockSpec((1, tl, md), lambda b, i: (b, i, 0)),
        out_shape=jax.ShapeDtypeStruct((bsz, seq, md), BF16),
        compiler_params=_cparams("arbitrary", "arbitrary"),
    )(h_f, h_b, z, head_norm.reshape(1, md))


def s5_matrices(lam_re, lam_im, log_step, b_re, b_im, c_re, c_im):
    tt = S5_TILE
    lam = lax.complex(lam_re.astype(F32), lam_im.astype(F32))
    step = jnp.exp(log_step.astype(F32))[..., None]
    lam_dt = lam * step
    lam_bar = jnp.exp(lam_dt)
    b_mat = lax.complex(b_re.astype(F32), b_im.astype(F32))
    b_bar = ((lam_bar - 1.0) / lam)[..., None] * b_mat[None]
    c_mat = lax.complex(c_re.astype(F32), c_im.astype(F32))
    ks = jnp.arange(tt + 1, dtype=F32)
    powers = jnp.exp(lam_dt[None] * ks[:, None, None, None])
    hp = lax.Precision.HIGHEST
    kern = jnp.real(jnp.einsum("gop,kdgp,dgpc->kdgoc", c_mat, powers[:tt], b_bar, precision=hp))
    s_idx = jnp.arange(tt)[:, None]
    t_idx = jnp.arange(tt)[None, :]
    lag_f = t_idx - s_idx
    m_f = jnp.where((lag_f >= 0)[..., None, None, None], kern[jnp.clip(lag_f, 0, tt - 1), 0], 0.0)
    m_b = jnp.where((lag_f <= 0)[..., None, None, None], kern[jnp.clip(-lag_f, 0, tt - 1), 1], 0.0)
    toep = (m_f + m_b).transpose(2, 0, 4, 1, 3)
    g_n, p_n, c_n = b_mat.shape
    toep = toep.reshape(g_n, tt * c_n, tt * c_n)
    pw_f = powers[tt - 1 - jnp.arange(tt), 0]
    pw_b = powers[jnp.arange(tt), 1]
    bin_f = pw_f[..., None] * b_bar[0][None]
    bin_b = pw_b[..., None] * b_bar[1][None]

    def lay_in(bc):
        bc = bc.transpose(1, 0, 3, 2).reshape(g_n, tt * c_n, p_n)
        re, im = jnp.real(bc), jnp.imag(bc)
        return jnp.concatenate([re, im, im, re], axis=-1)

    b_in = jnp.concatenate([lay_in(bin_f), lay_in(bin_b)], axis=-1)
    d_f = c_mat[None] * powers[1 + jnp.arange(tt), 0][:, :, None, :]
    d_b = c_mat[None] * powers[tt - jnp.arange(tt), 1][:, :, None, :]

    def lay_out(dc):
        dc = dc.transpose(1, 3, 0, 2).reshape(g_n, p_n, tt * c_n)
        return jnp.concatenate([jnp.real(dc), -jnp.imag(dc)], axis=1)

    c_out = jnp.concatenate([lay_out(d_f), lay_out(d_b)], axis=1)
    lt = powers[tt]
    lr, li = jnp.real(lt), jnp.imag(lt)
    lam_t = jnp.stack(
        [
            jnp.concatenate([lr[0], lr[0]], -1),
            jnp.concatenate([-li[0], li[0]], -1),
            jnp.concatenate([lr[1], lr[1]], -1),
            jnp.concatenate([-li[1], li[1]], -1),
        ],
        axis=1,
    )
    return toep.astype(BF16), b_in.astype(BF16), c_out.astype(BF16), lam_t


def _s5_kernel(bsz, u_ref, toep_ref, bin_ref, cout_ref, lam_ref, s0_ref, y_ref, s1_ref, v_s, st_s):
    u = u_ref[0].astype(BF16)
    v_s[...] = jnp.dot(u, bin_ref[0], preferred_element_type=F32)
    nk = u.shape[0] // bsz
    p2 = lam_ref.shape[2]
    lam = lam_ref[0]
    a_f, b_f, a_b, b_b = lam[0:1], lam[1:2], lam[2:3], lam[3:4]
    s0 = s0_ref[0]

    def body(k, carry):
        sf, sfx, sb, sbx = carry
        rf = pl.multiple_of(k * bsz, bsz)
        rb = pl.multiple_of((nk - 1 - k) * bsz, bsz)
        st_s[pl.ds(rf, bsz), 0:p2] = sf
        st_s[pl.ds(rb, bsz), p2 : 2 * p2] = sb
        vf = v_s[pl.ds(rf, bsz), 0 : 2 * p2]
        vb = v_s[pl.ds(rb, bsz), 2 * p2 : 4 * p2]
        sf_n = a_f * sf + b_f * sfx + vf[:, 0:p2]
        sfx_n = a_f * sfx - b_f * sf + vf[:, p2 : 2 * p2]
        sb_n = a_b * sb + b_b * sbx + vb[:, 0:p2]
        sbx_n = a_b * sbx - b_b * sb + vb[:, p2 : 2 * p2]
        return sf_n, sfx_n, sb_n, sbx_n

    carry = (s0[:, 0:p2], s0[:, p2 : 2 * p2], s0[:, 2 * p2 : 3 * p2], s0[:, 3 * p2 : 4 * p2])
    sf, sfx, sb, sbx = lax.fori_loop(0, nk, body, carry)
    s1_ref[0] = jnp.concatenate([sf, sfx, sb, sbx], axis=-1)
    y = jnp.dot(u, toep_ref[0], preferred_element_type=F32)
    y += jnp.dot(st_s[...].astype(BF16), cout_ref[0], preferred_element_type=F32)
    y_ref[0] = y


def s5_bidir(u_rows, mats, s0):
    toep, b_in, c_out, lam_t = mats
    g_n, rows, width = u_rows.shape
    bsz = s0.shape[1]
    p8 = b_in.shape[2]
    p2 = p8 // 4
    return pl.pallas_call(
        functools.partial(_s5_kernel, bsz),
        grid=(g_n,),
        in_specs=[
            pl.BlockSpec((1, rows, width), lambda g: (g, 0, 0)),
            pl.BlockSpec((1, width, width), lambda g: (g, 0, 0)),
            pl.BlockSpec((1, width, p8), lambda g: (g, 0, 0)),
            pl.BlockSpec((1, 2 * p2, width), lambda g: (g, 0, 0)),
            pl.BlockSpec((1, 4, p2), lambda g: (g, 0, 0)),
            pl.BlockSpec((1, bsz, p8), lambda g: (g, 0, 0)),
        ],
        out_specs=[
            pl.BlockSpec((1, rows, width), lambda g: (g, 0, 0)),
            pl.BlockSpec((1, bsz, p8), lambda g: (g, 0, 0)),
        ],
        out_shape=[jax.ShapeDtypeStruct(u_rows.shape, F32), jax.ShapeDtypeStruct(s0.shape, F32)],
        scratch_shapes=[pltpu.VMEM((rows, p8), F32), pltpu.VMEM((rows, 2 * p2), F32)],
        compiler_params=_cparams("arbitrary"),
    )(u_rows, toep, b_in, c_out, lam_t, s0)


def _s5_glu_kernel(ys_ref, u_ref, d_ref, w_ref, o_ref):
    y = ys_ref[0] + d_ref[...] * u_ref[0]
    y = jax.nn.gelu(y)
    gate = jax.nn.sigmoid(jnp.dot(y.astype(BF16), w_ref[...], preferred_element_type=F32))
    o_ref[0] = (y * gate).astype(o_ref.dtype)


def s5_glu(ys, z, u_col_block, d_skip, w_glu):
    bsz, seq, sd = ys.shape
    tl = _tile(seq, 512)
    return pl.pallas_call(
        _s5_glu_kernel,
        grid=(bsz, seq // tl),
        in_specs=[
            pl.BlockSpec((1, tl, sd), lambda b, i: (b, i, 0)),
            pl.BlockSpec((1, tl, sd), lambda b, i: (b, i, u_col_block)),
            pl.BlockSpec((1, sd), lambda b, i: (0, 0)),
            pl.BlockSpec((sd, sd), lambda b, i: (0, 0)),
        ],
        out_specs=pl.BlockSpec((1, tl, sd), lambda b, i: (b, i, 0)),
        out_shape=jax.ShapeDtypeStruct((bsz, seq, sd), BF16),
        compiler_params=_cparams("arbitrary", "arbitrary"),
    )(ys, z, d_skip.reshape(1, sd), w_glu)


def _to_s5_rows(z, u_col0, sd):
    bsz, seq = z.shape[:2]
    g_n = sd // S5_GROUP
    u = z[:, :, u_col0 : u_col0 + sd].reshape(bsz, seq // S5_TILE, S5_TILE, g_n, S5_GROUP)
    return u.transpose(3, 1, 0, 2, 4).reshape(g_n, (seq // S5_TILE) * bsz, S5_TILE * S5_GROUP)


def _from_s5_rows(y, bsz):
    g_n, rows, width = y.shape
    nk = rows // bsz
    y = y.reshape(g_n, nk, bsz, S5_TILE, S5_GROUP).transpose(2, 1, 3, 0, 4)
    return y.reshape(bsz, nk * S5_TILE, g_n * S5_GROUP)


def _hybrid_prep(h, w_main, w_gates, conv_w, col_scale, md):
    bsz, seq, d = h.shape
    z = matmul([(h.reshape(bsz * seq, d), w_main)], F32).reshape(bsz, seq, -1)
    gates = matmul([(h.reshape(bsz * seq, d), w_gates)], F32, tn=LANES).reshape(bsz, seq, LANES)
    qk = qk_conv(z, conv_w, col_scale, 2 * md)
    return z, gates, qk


def hybrid_mixer(h_c, h_l, p, need_ctx):
    d = h_l.shape[2]
    md = p["head_norm"].shape[0]
    sd = d - md
    nh = MLSTM_HEADS
    dh = md // nh
    bsz = h_l.shape[0]
    g_n = sd // S5_GROUP
    v_blk, o_blk, u_blk = 2, 3, 4
    outs = []
    m_state = (
        jnp.zeros((bsz, 2 * nh, dh, dh), F32),
        jnp.zeros((bsz, 2 * nh, 8, dh), F32),
        jnp.zeros((bsz, 2 * nh, 8, LANES), F32),
    )
    s_state = jnp.zeros((g_n, bsz, 8 * p["s5_mats"][3].shape[2] // 2), F32)
    for h, want_out in ((h_c, need_ctx), (h_l, True)):
        z, gates, qk = _hybrid_prep(h, p["w_main"], p["w_gates"], p["conv"], p["col_scale"], md)
        seq = h.shape[1]
        h_f, h_b, m_state = mlstm_bidir(
            qk, jnp.swapaxes(qk, 1, 2), z, v_blk, gates, jnp.swapaxes(gates[:, :, : 4 * nh], 1, 2), p["b_gates"], m_state
        )
        ys_rows, s_state = s5_bidir(_to_s5_rows(z, u_blk * sd, sd), p["s5_mats"], s_state)
        if not want_out:
            outs.append(None)
            continue
        hm = mlstm_finish(h_f, h_b, z, o_blk, p["head_norm"])
        y = s5_glu(_from_s5_rows(ys_rows, bsz), z, u_blk, p["d_skip"], p["w_glu"])
        outs.append((hm.reshape(bsz * seq, md), y.reshape(bsz * seq, sd)))
    return outs


def kernel(x, c, ctx, c_ctx, ada_w, ada_b, norm_mix, norm_ffn, norm_final, hy_w_in, hy_b_gates, hy_conv, hy_head_norm, s5_lam_re, s5_lam_im, s5_log_step, s5_b_re, s5_b_im, s5_c_re, s5_c_im, s5_d, s5_w_glu, hy_w_out, da_w_qkv, da_lambda, da_head_norm, da_w_o, ec_router, ec_w_gate, ec_w_up, ec_w_down):
    bsz, seq, d = x.shape
    lc = ctx.shape[1]
    depth = ada_w.shape[0]
    md = hy_head_norm.shape[1]
    sd = d - md
    nh = MLSTM_HEADS
    hd = d // (2 * DIFF_HEADS)

    rows = -(-(bsz + 1) // 8) * 8
    cond = jnp.zeros((rows, d), F32).at[:bsz].set(c).at[bsz].set(c_ctx)
    mod = ada_modulation(cond, ada_w, ada_b)
    mod_l = mod[:, :bsz].reshape(depth, bsz, ADA_CHUNKS, 1, d)
    mod_c = jnp.broadcast_to(mod[:, bsz : bsz + 1].reshape(depth, 1, ADA_CHUNKS, 1, d), (depth, bsz, ADA_CHUNKS, 1, d))

    n_rows = seq // GRID_W
    row_pos = jnp.repeat(jnp.arange(n_rows, dtype=F32), GRID_W)
    col_pos = jnp.tile(jnp.arange(GRID_W, dtype=F32), n_rows)
    axis_dim = hd // 2
    inv_freq = ROPE_BASE ** (-jnp.arange(0, axis_dim, 2, dtype=F32) / axis_dim)
    ar = row_pos[:, None] * inv_freq[None]
    ac = col_pos[:, None] * inv_freq[None]
    ang = jnp.concatenate([ar, ar, ac, ac], axis=-1)
    quarter = hd // 4
    sign = jnp.where((jnp.arange(hd) % (2 * quarter)) < quarter, -1.0, 1.0)
    cos_t, sin_t = jnp.cos(ang), jnp.sin(ang) * sign

    for layer in range(depth):
        need_ctx = layer < depth - 1
        j = layer // 2
        ml = [mod_l[layer, :, i] for i in range(ADA_CHUNKS)]
        mc = [mod_c[layer, :, i] for i in range(ADA_CHUNKS)]
        (h_l,) = norm_modulate(x, norm_mix[layer], ml[1], ml[0], [BF16])
        (h_c,) = norm_modulate(ctx, norm_mix[layer], mc[1], mc[0], [BF16])
        if layer % 2 == 0:
            w_in = hy_w_in[j]
            g0 = 4 * md
            g1 = g0 + 4 * nh
            w_main = jnp.concatenate([w_in[:, :g0], w_in[:, g1:]], axis=1).astype(BF16)
            w_gates = jnp.zeros((d, LANES), F32).at[:, : 4 * nh].set(w_in[:, g0:g1]).astype(BF16)
            col_scale = jnp.concatenate([jnp.ones((md,), F32), jnp.full((md,), (md // nh) ** -0.5, F32)]).reshape(1, 2 * md)
            p = dict(
                w_main=w_main, w_gates=w_gates, conv=hy_conv[j], col_scale=col_scale, b_gates=hy_b_gates[j],
                head_norm=hy_head_norm[j], d_skip=s5_d[j], w_glu=s5_w_glu[j].astype(BF16),
                s5_mats=s5_matrices(s5_lam_re[j], s5_lam_im[j], s5_log_step[j], s5_b_re[j], s5_b_im[j], s5_c_re[j], s5_c_im[j]),
            )
            mix_c, mix_l = hybrid_mixer(h_c, h_l, p, need_ctx)
            w_out = hy_w_out[j].astype(BF16)
            pairs_l = [(mix_l[0], w_out[:md]), (mix_l[1], w_out[md:])]
            pairs_c = [(mix_c[0], w_out[:md]), (mix_c[1], w_out[md:])] if need_ctx else None
        else:
            lam_init = 0.8 - 0.6 * math.exp(-0.3 * layer)
            lv = da_lambda[j].astype(F32)
            lam = jnp.exp(jnp.sum(lv[0] * lv[1])) - jnp.exp(jnp.sum(lv[2] * lv[3])) + lam_init
            w_qkv = da_w_qkv[j].astype(BF16)
            qkv_l = matmul([(h_l.reshape(bsz * seq, d), w_qkv)], BF16, rope=(cos_t, sin_t, 2 * d, seq)).reshape(bsz, seq, 3 * d)
            qkv_c = matmul([(h_c.reshape(bsz * lc, d), w_qkv)], BF16).reshape(bsz, lc, 3 * d)
            o_l = diff_attention(qkv_l, [qkv_c, qkv_l], lam, da_head_norm[j], 1.0 - lam_init)
            w_o = da_w_o[j].astype(BF16)
            pairs_l = [(o_l.reshape(bsz * seq, d), w_o)]
            pairs_c = None
            if need_ctx:
                o_c = diff_attention(qkv_c, [qkv_c], lam, da_head_norm[j], 1.0 - lam_init)
                pairs_c = [(o_c.reshape(bsz * lc, d), w_o)]
        x = matmul(pairs_l, F32, residual=x.reshape(bsz * seq, d), gate=ml[2], rows_per_batch=seq).reshape(bsz, seq, d)
        w_rt = ec_router[layer].T.astype(BF16)
        w_g, w_u, w_d = (w[layer].astype(BF16) for w in (ec_w_gate, ec_w_up, ec_w_down))
        hb, hf = norm_modulate(x, norm_ffn[layer], ml[4], ml[3], [BF16, F32])
        x = expert_choice_ffn(x, hb, hf, ml[5], w_rt, w_g, w_u, w_d)
        if need_ctx:
            ctx = matmul(pairs_c, F32, residual=ctx.reshape(bsz * lc, d), gate=mc[2], rows_per_batch=lc).reshape(bsz, lc, d)
            hb, hf = norm_modulate(ctx, norm_ffn[layer], mc[4], mc[3], [BF16, F32])
            ctx = expert_choice_ffn(ctx, hb, hf, mc[5], w_rt, w_g, w_u, w_d)
    return final_norm(x, norm_final)
```

```python
import functools
import math

import jax
import jax.numpy as jnp
from jax import lax
from jax.experimental import pallas as pl
from jax.experimental.pallas import tpu as pltpu

F32 = jnp.float32
BF16 = jnp.bfloat16

GRID_W = 64
EPS = 1e-6
ADA_CHUNKS = 6
MLSTM_HEADS = 4
QK_CONV = 3
S5_GROUP = 16
DIFF_HEADS = 8
ROPE_BASE = 10000.0
CAPACITY_FACTOR = 2

MLSTM_TILE = 256
S5_TILE = 16
VMEM_LIMIT_BYTES = 56 * 1024 * 1024
ROW_TILE = 1024
COL_TILE = 1024
ATTN_Q_TILE = 1024
ATTN_SUB_TILE = 256
ATTN_SCORE_SLOTS = 3
LANES = 128


def _cparams(*sem):
    return pltpu.CompilerParams(dimension_semantics=sem, vmem_limit_bytes=VMEM_LIMIT_BYTES)


def _tile(n, pref):
    t = min(n, pref)
    while n % t:
        t //= 2
    return t


def _ada_kernel(c_ref, w_ref, b_ref, o_ref):
    cv = c_ref[...]
    a = (cv * jax.nn.sigmoid(cv)).astype(BF16)
    o_ref[0] = jnp.dot(a, w_ref[0].astype(BF16), preferred_element_type=F32) + b_ref[0]


def ada_modulation(cond, ada_w, ada_b):
    depth, d, n = ada_w.shape
    rows = cond.shape[0]
    tn = _tile(n, COL_TILE)
    return pl.pallas_call(
        _ada_kernel,
        grid=(depth, n // tn),
        in_specs=[
            pl.BlockSpec((rows, d), lambda l, j: (0, 0)),
            pl.BlockSpec((1, d, tn), lambda l, j: (l, 0, j)),
            pl.BlockSpec((1, 1, tn), lambda l, j: (l, 0, j)),
        ],
        out_specs=pl.BlockSpec((1, rows, tn), lambda l, j: (l, 0, j)),
        out_shape=jax.ShapeDtypeStruct((depth, rows, n), F32),
        compiler_params=_cparams("arbitrary", "arbitrary"),
    )(cond, ada_w, ada_b.reshape(depth, 1, n))


def _normmod_kernel(x_ref, g_ref, sc_ref, sh_ref, *o_refs):
    x = x_ref[0]
    y = x * lax.rsqrt(jnp.mean(x * x, axis=-1, keepdims=True) + EPS) * g_ref[...]
    y = y * (1.0 + sc_ref[0]) + sh_ref[0]
    for o_ref in o_refs:
        o_ref[0] = y.astype(o_ref.dtype)


def norm_modulate(x, g, scale, shift, dtypes):
    bsz, seq, d = x.shape
    tl = _tile(seq, 512)
    return pl.pallas_call(
        _normmod_kernel,
        grid=(bsz, seq // tl),
        in_specs=[
            pl.BlockSpec((1, tl, d), lambda b, i: (b, i, 0)),
            pl.BlockSpec((1, d), lambda b, i: (0, 0)),
            pl.BlockSpec((1, 1, d), lambda b, i: (b, 0, 0)),
            pl.BlockSpec((1, 1, d), lambda b, i: (b, 0, 0)),
        ],
        out_specs=[pl.BlockSpec((1, tl, d), lambda b, i: (b, i, 0)) for _ in dtypes],
        out_shape=[jax.ShapeDtypeStruct(x.shape, dt) for dt in dtypes],
        compiler_params=_cparams("arbitrary", "arbitrary"),
    )(x, g.reshape(1, d), scale, shift)


def _final_norm_kernel(x_ref, g_ref, o_ref):
    x = x_ref[0]
    o_ref[0] = x * lax.rsqrt(jnp.mean(x * x, axis=-1, keepdims=True) + EPS) * g_ref[...]


def final_norm(x, g):
    bsz, seq, d = x.shape
    tl = _tile(seq, 512)
    return pl.pallas_call(
        _final_norm_kernel,
        grid=(bsz, seq // tl),
        in_specs=[pl.BlockSpec((1, tl, d), lambda b, i: (b, i, 0)), pl.BlockSpec((1, d), lambda b, i: (0, 0))],
        out_specs=pl.BlockSpec((1, tl, d), lambda b, i: (b, i, 0)),
        out_shape=jax.ShapeDtypeStruct(x.shape, F32),
        compiler_params=_cparams("arbitrary", "arbitrary"),
    )(x, g.reshape(1, d))


def _mm_kernel(n_pairs, *refs):
    o_ref = refs[2 * n_pairs]
    acc = jnp.dot(refs[0][...], refs[1][...], preferred_element_type=F32)
    for p in range(1, n_pairs):
        acc += jnp.dot(refs[2 * p][...], refs[2 * p + 1][...], preferred_element_type=F32)
    o_ref[...] = acc.astype(o_ref.dtype)


def _mm_colscale_kernel(a_ref, w_ref, cs_ref, o_ref):
    acc = jnp.dot(a_ref[...], w_ref[...], preferred_element_type=F32)
    o_ref[...] = (acc * cs_ref[...]).astype(o_ref.dtype)


def _mm_residual_kernel(n_pairs, *refs):
    r_ref, g_ref, o_ref = refs[2 * n_pairs : 2 * n_pairs + 3]
    acc = jnp.dot(refs[0][...], refs[1][...], preferred_element_type=F32)
    for p in range(1, n_pairs):
        acc += jnp.dot(refs[2 * p][...], refs[2 * p + 1][...], preferred_element_type=F32)
    o_ref[...] = r_ref[...] + g_ref[0] * acc


def _mm_rope_kernel(n_rope_tiles, quarter, a_ref, w_ref, cs_ref, cos_ref, sin_ref, o_ref):
    acc = jnp.dot(a_ref[...], w_ref[...], preferred_element_type=F32) * cs_ref[...]
    j = pl.program_id(1)

    @pl.when(j >= n_rope_tiles)
    def _():
        o_ref[...] = acc.astype(o_ref.dtype)

    @pl.when(j < n_rope_tiles)
    def _():
        cos = cos_ref[...]
        sin = sin_ref[...]
        hd = cos.shape[1]
        lane = lax.broadcasted_iota(jnp.int32, (acc.shape[0], hd), 1)
        first = (lane % (2 * quarter)) < quarter
        for g in range(acc.shape[1] // hd):
            xg = acc[:, g * hd : (g + 1) * hd]
            up = pltpu.roll(xg, hd - quarter, 1)
            dn = pltpu.roll(xg, quarter, 1)
            o_ref[:, g * hd : (g + 1) * hd] = (xg * cos + jnp.where(first, up, dn) * sin).astype(o_ref.dtype)


def matmul(pairs, out_dtype, *, tm=ROW_TILE, tn=COL_TILE, residual=None, gate=None, rows_per_batch=None, rope=None,
           col_scale=None):
    m = pairs[0][0].shape[0]
    n = pairs[0][1].shape[1]
    if rope is not None:
        rows_per_batch = rope[3]
    tm = _tile(m if rows_per_batch is None else rows_per_batch, tm)
    tn = _tile(n, tn)
    grid = (m // tm, n // tn)
    in_specs, args = [], []
    for a, w in pairs:
        k = a.shape[1]
        in_specs += [pl.BlockSpec((tm, k), lambda i, j: (i, 0)), pl.BlockSpec((k, tn), lambda i, j: (0, j))]
        args += [a, w]
    out_spec = pl.BlockSpec((tm, tn), lambda i, j: (i, j))
    aliases = {}
    if residual is not None:
        bpt = rows_per_batch // tm
        in_specs += [pl.BlockSpec((tm, tn), lambda i, j: (i, j)), pl.BlockSpec((1, 1, tn), lambda i, j: (i // bpt, 0, j))]
        args += [residual, gate]
        aliases = {2 * len(pairs): 0}
        body = functools.partial(_mm_residual_kernel, len(pairs))
    elif rope is not None:
        cos, sin, n_rope_cols, rpb = rope
        hd = cos.shape[1]
        bpt = rpb // tm
        in_specs += [
            pl.BlockSpec((1, tn), lambda i, j: (0, j)),
            pl.BlockSpec((tm, hd), lambda i, j: (i % bpt, 0)),
            pl.BlockSpec((tm, hd), lambda i, j: (i % bpt, 0)),
        ]
        args += [col_scale, cos, sin]
        body = functools.partial(_mm_rope_kernel, n_rope_cols // tn, hd // 4)
    elif col_scale is not None:
        in_specs.append(pl.BlockSpec((1, tn), lambda i, j: (0, j)))
        args.append(col_scale)
        body = _mm_colscale_kernel
    else:
        body = functools.partial(_mm_kernel, len(pairs))
    return pl.pallas_call(
        body,
        grid=grid,
        in_specs=in_specs,
        out_specs=out_spec,
        out_shape=jax.ShapeDtypeStruct((m, n), out_dtype),
        input_output_aliases=aliases,
        compiler_params=_cparams("arbitrary", "arbitrary"),
    )(*args)


def _attn_kernel(n_seg, seg_off, hd, sub, out_scale, lam_ref, q_ref, *refs):
    kv_refs = refs[: 2 * n_seg]
    g_ref, o_ref, s_scr, p_scr = refs[2 * n_seg : 2 * n_seg + 4]
    lam = lam_ref[0]
    nt = (((1,), (1,)), ((), ()))
    n_sub = q_ref.shape[1] // sub
    n_chain = 2 * n_sub
    n_s, n_p = s_scr.shape[0], p_scr.shape[0]
    row_max, row_sum, outs = {}, {}, {}

    def stage_scores(c):
        r, i = divmod(c, 2)
        qi = q_ref[0, r * sub : (r + 1) * sub, i * hd : (i + 1) * hd]
        m = None
        for s in range(n_seg):
            lo, hi = seg_off[s], seg_off[s + 1]
            sc = lax.dot_general(qi, kv_refs[2 * s][0, :, i * hd : (i + 1) * hd], nt, preferred_element_type=F32)
            s_scr[c % n_s, :, lo:hi] = sc
            ms = sc.max(axis=-1, keepdims=True)
            m = ms if m is None else jnp.maximum(m, ms)
        row_max[c] = m

    def stage_probs(c):
        p = jnp.exp2(s_scr[c % n_s] - row_max[c])
        row_sum[c] = p.sum(axis=-1, keepdims=True)
        p_scr[c % n_p] = p.astype(BF16)

    def stage_values(c):
        acc = None
        for s in range(n_seg):
            lo, hi = seg_off[s], seg_off[s + 1]
            part = jnp.dot(p_scr[c % n_p, :, lo:hi], kv_refs[2 * s + 1][0], preferred_element_type=F32)
            acc = part if acc is None else acc + part
        outs[c] = acc / row_sum[c]
        if c % 2 == 1:
            r = c // 2
            o = outs.pop(c - 1) - lam * outs.pop(c)
            o = o * lax.rsqrt(jnp.mean(o * o, axis=-1, keepdims=True) + EPS) * g_ref[...] * out_scale
            o_ref[0, r * sub : (r + 1) * sub, :] = o.astype(o_ref.dtype)

    for t in range(n_chain + 2):
        if t < n_chain:
            stage_scores(t)
        if 0 <= t - 1 < n_chain:
            stage_probs(t - 1)
        if 0 <= t - 2 < n_chain:
            stage_values(t - 2)


def diff_attention(q_src, kv_srcs, lam, head_norm, out_scale):
    bsz, lq, d3 = q_src.shape
    d = d3 // 3
    hw = d // DIFF_HEADS
    hd = hw // 2
    tq = _tile(lq, ATTN_Q_TILE)
    sub = _tile(tq, ATTN_SUB_TILE)
    seg_off = [0]
    for src in kv_srcs:
        seg_off.append(seg_off[-1] + src.shape[1])
    in_specs = [
        pl.BlockSpec(memory_space=pltpu.SMEM),
        pl.BlockSpec((1, tq, hw), lambda b, h, i: (b, i, h)),
    ]
    args = [lam.reshape(1), q_src]
    for src in kv_srcs:
        ls = src.shape[1]
        in_specs += [
            pl.BlockSpec((1, ls, hw), lambda b, h, i: (b, 0, DIFF_HEADS + h)),
            pl.BlockSpec((1, ls, hw), lambda b, h, i: (b, 0, 2 * DIFF_HEADS + h)),
        ]
        args += [src, src]
    in_specs.append(pl.BlockSpec((1, hw), lambda b, h, i: (0, h)))
    args.append(head_norm.reshape(1, d))
    return pl.pallas_call(
        functools.partial(_attn_kernel, len(kv_srcs), tuple(seg_off), hd, sub, out_scale),
        grid=(bsz, DIFF_HEADS, lq // tq),
        in_specs=in_specs,
        out_specs=pl.BlockSpec((1, tq, hw), lambda b, h, i: (b, i, h)),
        out_shape=jax.ShapeDtypeStruct((bsz, lq, d), BF16),
        scratch_shapes=[
            pltpu.VMEM((ATTN_SCORE_SLOTS, sub, seg_off[-1]), F32),
            pltpu.VMEM((ATTN_SCORE_SLOTS, sub, seg_off[-1]), BF16),
        ],
        compiler_params=_cparams("arbitrary", "arbitrary", "arbitrary"),
    )(*args)


def _router_kernel(h_ref, w_ref, o_ref):
    nt = (((1,), (1,)), ((), ()))
    logits = lax.dot_general(w_ref[...], h_ref[0], nt, preferred_element_type=F32)
    z = jnp.exp(logits - logits.max(axis=0, keepdims=True))
    o_ref[0] = z / z.sum(axis=0, keepdims=True)


def router_affinity(h, w_router_t):
    bsz, seq, d = h.shape
    ne = w_router_t.shape[0]
    tl = _tile(seq, 1024)
    return pl.pallas_call(
        _router_kernel,
        grid=(bsz, seq // tl),
        in_specs=[pl.BlockSpec((1, tl, d), lambda b, i: (b, i, 0)), pl.BlockSpec((ne, d), lambda b, i: (0, 0))],
        out_specs=pl.BlockSpec((1, ne, tl), lambda b, i: (b, 0, i)),
        out_shape=jax.ShapeDtypeStruct((bsz, ne, seq), F32),
        compiler_params=_cparams("arbitrary", "arbitrary"),
    )(h, w_router_t)


def _row_copy(src, src_row, dst, dst_row, sem):
    return pltpu.make_async_copy(src.at[pl.ds(src_row, 1)], dst.at[pl.ds(dst_row, 1)], sem)


DMA_UNROLL = 8


def _moe_kernel(cap, nb, idx_ref, idx_next_ref, gate_ref, mod_ref, h_hbm, x_in, wg_ref, wu_ref, wd_ref, x_out,
                xs_buf, xr_buf, sems):
    del x_in
    e, b = pl.program_id(0), pl.program_id(1)
    n = e * nb + b
    n_last = pl.num_programs(0) * nb - 1
    slot = n % 2
    other = 1 - slot
    sem_h, sem_in, sem_out = 0, 1, 2

    def start_rows(rows_of, idx, buf_slot, sem, to_hbm=False):
        def body(c, carry):
            t = idx[0, 0, c]
            if to_hbm:
                _row_copy(buf_slot, c, rows_of, t, sem).start()
            else:
                _row_copy(rows_of, t, buf_slot, c, sem).start()
            return carry

        lax.fori_loop(0, cap, body, 0, unroll=min(DMA_UNROLL, cap))

    def wait_all(buf_slot, sem):
        pltpu.make_async_copy(h_hbm.at[0, pl.ds(0, cap)], buf_slot, sem).wait()

    @pl.when(n == 0)
    def _():
        start_rows(h_hbm.at[b], idx_ref, xs_buf.at[slot], sems.at[sem_h, slot])

    @pl.when(n < n_last)
    def _():
        start_rows(h_hbm.at[(b + 1) % nb], idx_next_ref, xs_buf.at[other], sems.at[sem_h, other])

    start_rows(x_out.at[b], idx_ref, xr_buf.at[slot], sems.at[sem_in, slot])

    wait_all(xs_buf.at[slot], sems.at[sem_h, slot])
    xs = xs_buf[slot].astype(BF16)
    gt = jnp.dot(xs, wg_ref[0], preferred_element_type=F32)
    up = jnp.dot(xs, wu_ref[0], preferred_element_type=F32)
    hid = (gt * jax.nn.sigmoid(gt) * up).astype(BF16)
    out = jnp.dot(hid, wd_ref[0], preferred_element_type=F32) * gate_ref[0] * mod_ref[0]
    wait_all(xr_buf.at[slot], sems.at[sem_in, slot])
    xr_buf[slot] += out

    @pl.when(n > 0)
    def _():
        wait_all(xr_buf.at[other], sems.at[sem_out, other])

    start_rows(x_out.at[b], idx_ref, xr_buf.at[slot], sems.at[sem_out, slot], to_hbm=True)

    @pl.when(n == n_last)
    def _():
        wait_all(xr_buf.at[slot], sems.at[sem_out, slot])


def moe_apply(x, h, idx, gate, mod, w_gate, w_up, w_down):
    bsz, seq, d = x.shape
    ne, cap = idx.shape[1], idx.shape[2]
    ff = w_gate.shape[2]
    idx_rows = idx.reshape(bsz * ne, 1, cap)

    def next_rows(e, b):
        nxt = jnp.minimum(e * bsz + b + 1, ne * bsz - 1)
        return ((nxt % bsz) * ne + nxt // bsz, 0, 0)

    single = pl.Buffered(1)
    return pl.pallas_call(
        functools.partial(_moe_kernel, cap, bsz),
        grid=(ne, bsz),
        in_specs=[
            pl.BlockSpec((1, 1, cap), lambda e, b: (b * ne + e, 0, 0), memory_space=pltpu.SMEM),
            pl.BlockSpec((1, 1, cap), next_rows, memory_space=pltpu.SMEM),
            pl.BlockSpec((1, cap, 1), lambda e, b: (b * ne + e, 0, 0)),
            pl.BlockSpec((1, 1, d), lambda e, b: (b, 0, 0)),
            pl.BlockSpec(memory_space=pl.ANY),
            pl.BlockSpec(memory_space=pl.ANY),
            pl.BlockSpec((1, d, ff), lambda e, b: (e, 0, 0), pipeline_mode=single),
            pl.BlockSpec((1, d, ff), lambda e, b: (e, 0, 0), pipeline_mode=single),
            pl.BlockSpec((1, ff, d), lambda e, b: (e, 0, 0), pipeline_mode=single),
        ],
        out_specs=pl.BlockSpec(memory_space=pl.ANY),
        out_shape=jax.ShapeDtypeStruct(x.shape, F32),
        scratch_shapes=[
            pltpu.VMEM((2, cap, d), F32),
            pltpu.VMEM((2, cap, d), F32),
            pltpu.SemaphoreType.DMA((3, 2)),
        ],
        input_output_aliases={5: 0},
        compiler_params=_cparams("arbitrary", "arbitrary"),
    )(idx_rows, idx_rows, gate.reshape(bsz * ne, cap, 1), mod, h, x, w_gate, w_up, w_down)


def expert_choice_ffn(x, h_bf16, h_f32, mod, w_router_t, w_gate, w_up, w_down):
    seq = x.shape[1]
    ne = w_router_t.shape[0]
    cap = CAPACITY_FACTOR * seq // ne
    aff = router_affinity(h_bf16, w_router_t)
    gate, idx = lax.top_k(aff, cap)
    return moe_apply(x, h_f32, idx.astype(jnp.int32), gate, mod, w_gate, w_up, w_down)


def _conv_kernel(z_ref, w_ref, s_ref, o_ref):
    x = z_ref[0]
    seq = x.shape[0]
    row = lax.broadcasted_iota(jnp.int32, x.shape, 0)
    prev = jnp.where(row == 0, 0.0, pltpu.roll(x, 1, 0))
    nxt = jnp.where(row == seq - 1, 0.0, pltpu.roll(x, seq - 1, 0))
    w = w_ref[...]
    y = prev * w[0:1] + x * w[1:2] + nxt * w[2:3]
    o_ref[0] = (y * jax.nn.sigmoid(y) * s_ref[...]).astype(o_ref.dtype)


def qk_conv(z, conv_w, col_scale, n_cols):
    bsz, seq = z.shape[:2]
    tc = 256
    return pl.pallas_call(
        _conv_kernel,
        grid=(bsz, n_cols // tc),
        in_specs=[
            pl.BlockSpec((1, seq, tc), lambda b, j: (b, 0, j)),
            pl.BlockSpec((QK_CONV, tc), lambda b, j: (0, j)),
            pl.BlockSpec((1, tc), lambda b, j: (0, j)),
        ],
        out_specs=pl.BlockSpec((1, seq, tc), lambda b, j: (b, 0, j)),
        out_shape=jax.ShapeDtypeStruct((bsz, seq, n_cols), BF16),
        compiler_params=_cparams("arbitrary", "arbitrary"),
    )(z, conv_w, col_scale)


def _log_sigmoid(x):
    return jnp.minimum(x, 0.0) - jnp.log(1.0 + jnp.exp(-jnp.abs(x)))


def _mlstm_kernel(nh, dh, *refs):
    dir_refs = (refs[0:6], refs[6:12])
    bias_ref, bias_t_ref, c0_ref, n0_ref, m0_ref = refs[12:17]
    h_outs = refs[17:19]
    c_out, n_out, m_out = refs[19:22]
    c_s, n_s, m_s = refs[22:25]
    kc = pl.program_id(1)
    last = pl.num_programs(1) - 1

    @pl.when(kc == 0)
    def _():
        c_s[...] = c0_ref[0]
        n_s[...] = n0_ref[0]
        m_s[...] = m0_ref[0]

    tt = dir_refs[0][0].shape[1]
    row = lax.broadcasted_iota(jnp.int32, (tt, tt), 0)
    col = lax.broadcasted_iota(jnp.int32, (tt, tt), 1)
    nt = (((1,), (1,)), ((), ()))
    hi = lax.Precision.HIGHEST
    for d in range(2):
        q_ref, k_ref, kt_ref, v_ref, g_ref, gt_ref = dir_refs[d]
        window = (col <= row) if d == 0 else (col >= row)
        tri = window.astype(F32)
        gates = g_ref[0] + bias_ref[...]
        gates_t = gt_ref[0] + bias_t_ref[...]
        b_cols = jnp.dot(tri, _log_sigmoid(gates), precision=hi, preferred_element_type=F32)
        b_rows = lax.dot_general(_log_sigmoid(gates_t), tri, nt, precision=hi, preferred_element_type=F32)
        edge = tt - 1 if d == 0 else 0
        for h in range(nh):
            ci = 2 * nh * d + h
            cf = ci + nh
            sl = slice(h * dh, (h + 1) * dh)
            st = d * nh + h
            i_col, i_row = gates[:, ci : ci + 1], gates_t[ci : ci + 1, :]
            b_col, b_row = b_cols[:, cf : cf + 1], b_rows[cf : cf + 1, :]
            m_prev = m_s[st][0:1, 0:1]
            log_d = jnp.where(window, b_col - b_row + i_row, -jnp.inf)
            inter = b_col + m_prev
            m_t = jnp.maximum(log_d.max(axis=-1, keepdims=True), inter)
            d_mat = jnp.exp(log_d - m_t)
            w_inter = jnp.exp(inter - m_t)
            qh = q_ref[0][:, sl]
            vh = v_ref[0][:, sl]
            s_mat = jnp.dot(qh, kt_ref[0][sl, :], preferred_element_type=F32) * d_mat
            c_t = c_s[st]
            n_row = n_s[st][0:1, :]
            num = jnp.dot(s_mat.astype(BF16), vh.astype(BF16), preferred_element_type=F32)
            num += w_inter * jnp.dot(qh, c_t.astype(BF16), preferred_element_type=F32)
            den = s_mat.sum(axis=-1, keepdims=True) + w_inter * (qh.astype(F32) * n_row).sum(axis=-1, keepdims=True)
            h_outs[d][0, :, sl] = num / jnp.maximum(jnp.abs(den), jnp.exp(-m_t))
            b_last = b_col[edge : edge + 1, :]
            ws_col = b_last - b_col + i_col
            m_new = jnp.maximum(b_last + m_prev, ws_col.max(axis=0, keepdims=True))
            decay = jnp.exp(b_last + m_prev - m_new)
            ws_col = jnp.exp(ws_col - m_new)
            c_s[st] = decay * c_t + jnp.dot(kt_ref[0][sl, :], (ws_col * vh).astype(BF16), preferred_element_type=F32)
            n_new = decay * n_row + (ws_col * k_ref[0][:, sl].astype(F32)).sum(axis=0, keepdims=True)
            n_s[st] = jnp.broadcast_to(n_new, n_s.shape[1:])
            m_s[st] = jnp.broadcast_to(m_new, m_s.shape[1:])

    @pl.when(kc == last)
    def _():
        c_out[0] = c_s[...]
        n_out[0] = n_s[...]
        m_out[0] = m_s[...]


def mlstm_bidir(qk, qk_t, z, v_col_block, gates, gates_t, bias, state):
    bsz, seq, m2 = qk.shape
    md = m2 // 2
    nh = MLSTM_HEADS
    dh = md // nh
    tt = _tile(seq, MLSTM_TILE)
    nc = seq // tt
    fwd = lambda b, k: k
    bwd = lambda b, k: nc - 1 - k
    in_specs, args = [], []
    for ck in (fwd, bwd):
        in_specs += [
            pl.BlockSpec((1, tt, md), lambda b, k, ck=ck: (b, ck(b, k), 0)),
            pl.BlockSpec((1, tt, md), lambda b, k, ck=ck: (b, ck(b, k), 1)),
            pl.BlockSpec((1, md, tt), lambda b, k, ck=ck: (b, 1, ck(b, k))),
            pl.BlockSpec((1, tt, md), lambda b, k, ck=ck: (b, ck(b, k), v_col_block)),
            pl.BlockSpec((1, tt, LANES), lambda b, k, ck=ck: (b, ck(b, k), 0)),
            pl.BlockSpec((1, 4 * nh, tt), lambda b, k, ck=ck: (b, 0, ck(b, k))),
        ]
        args += [qk, qk, qk_t, z, gates, gates_t]
    c0, n0, m0 = state
    ns = 2 * nh
    in_specs += [
        pl.BlockSpec((1, LANES), lambda b, k: (0, 0)),
        pl.BlockSpec((4 * nh, 1), lambda b, k: (0, 0)),
        pl.BlockSpec((1, ns, dh, dh), lambda b, k: (b, 0, 0, 0)),
        pl.BlockSpec((1, ns, 8, dh), lambda b, k: (b, 0, 0, 0)),
        pl.BlockSpec((1, ns, 8, LANES), lambda b, k: (b, 0, 0, 0)),
    ]
    bias_row = jnp.zeros((1, LANES), F32).at[0, : 4 * nh].set(bias)
    args += [bias_row, bias.reshape(4 * nh, 1), c0, n0, m0]
    out_specs = [
        pl.BlockSpec((1, tt, md), lambda b, k: (b, k, 0)),
        pl.BlockSpec((1, tt, md), lambda b, k: (b, nc - 1 - k, 0)),
        pl.BlockSpec((1, ns, dh, dh), lambda b, k: (b, 0, 0, 0)),
        pl.BlockSpec((1, ns, 8, dh), lambda b, k: (b, 0, 0, 0)),
        pl.BlockSpec((1, ns, 8, LANES), lambda b, k: (b, 0, 0, 0)),
    ]
    out_shape = [
        jax.ShapeDtypeStruct((bsz, seq, md), F32),
        jax.ShapeDtypeStruct((bsz, seq, md), F32),
        jax.ShapeDtypeStruct(c0.shape, F32),
        jax.ShapeDtypeStruct(n0.shape, F32),
        jax.ShapeDtypeStruct(m0.shape, F32),
    ]
    h_f, h_b, c1, n1, m1 = pl.pallas_call(
        functools.partial(_mlstm_kernel, nh, dh),
        grid=(bsz, nc),
        in_specs=in_specs,
        out_specs=out_specs,
        out_shape=out_shape,
        scratch_shapes=[pltpu.VMEM((ns, dh, dh), F32), pltpu.VMEM((ns, 8, dh), F32), pltpu.VMEM((ns, 8, LANES), F32)],
        compiler_params=_cparams("arbitrary", "arbitrary"),
    )(*args)
    return h_f, h_b, (c1, n1, m1)


def _mlstm_finish_kernel(nh, hf_ref, hb_ref, o_ref, g_ref, out_ref):
    hm = hf_ref[0] + hb_ref[0]
    dh = hm.shape[1] // nh
    gate = jax.nn.sigmoid(o_ref[0])
    g = g_ref[...]
    for h in range(nh):
        sl = slice(h * dh, (h + 1) * dh)
        xh = hm[:, sl]
        y = xh * lax.rsqrt(jnp.mean(xh * xh, axis=-1, keepdims=True) + EPS) * g[:, sl]
        out_ref[0, :, sl] = (y * gate[:, sl]).astype(out_ref.dtype)


def mlstm_finish(h_f, h_b, z, o_col_block, head_norm):
    bsz, seq, md = h_f.shape
    tl = _tile(seq, 512)
    return pl.pallas_call(
        functools.partial(_mlstm_finish_kernel, MLSTM_HEADS),
        grid=(bsz, seq // tl),
        in_specs=[
            pl.BlockSpec((1, tl, md), lambda b, i: (b, i, 0)),
            pl.BlockSpec((1, tl, md), lambda b, i: (b, i, 0)),
            pl.BlockSpec((1, tl, md), lambda b, i: (b, i, o_col_block)),
            pl.BlockSpec((1, md), lambda b, i: (0, 0)),
        ],
        out_specs=pl.BlockSpec((1, tl, md), lambda b, i: (b, i, 0)),
        out_shape=jax.ShapeDtypeStruct((bsz, seq, md), BF16),
        compiler_params=_cparams("arbitrary", "arbitrary"),
    )(h_f, h_b, z, head_norm.reshape(1, md))


def s5_matrices(lam_re, lam_im, log_step, b_re, b_im, c_re, c_im):
    tt = S5_TILE
    lam = lax.complex(lam_re.astype(F32), lam_im.astype(F32))
    step = jnp.exp(log_step.astype(F32))[..., None]
    lam_dt = lam * step
    lam_bar = jnp.exp(lam_dt)
    b_mat = lax.complex(b_re.astype(F32), b_im.astype(F32))
    b_bar = ((lam_bar - 1.0) / lam)[..., None] * b_mat[None]
    c_mat = lax.complex(c_re.astype(F32), c_im.astype(F32))
    ks = jnp.arange(tt + 1, dtype=F32)
    powers = jnp.exp(lam_dt[None] * ks[:, None, None, None])
    hp = lax.Precision.HIGHEST
    kern = jnp.real(jnp.einsum("gop,kdgp,dgpc->kdgoc", c_mat, powers[:tt], b_bar, precision=hp))
    s_idx = jnp.arange(tt)[:, None]
    t_idx = jnp.arange(tt)[None, :]
    lag_f = t_idx - s_idx
    m_f = jnp.where((lag_f >= 0)[..., None, None, None], kern[jnp.clip(lag_f, 0, tt - 1), 0], 0.0)
    m_b = jnp.where((lag_f <= 0)[..., None, None, None], kern[jnp.clip(-lag_f, 0, tt - 1), 1], 0.0)
    toep = (m_f + m_b).transpose(2, 0, 4, 1, 3)
    g_n, p_n, c_n = b_mat.shape
    toep = toep.reshape(g_n, tt * c_n, tt * c_n)
    pw_f = powers[tt - 1 - jnp.arange(tt), 0]
    pw_b = powers[jnp.arange(tt), 1]
    bin_f = pw_f[..., None] * b_bar[0][None]
    bin_b = pw_b[..., None] * b_bar[1][None]

    def lay_in(bc):
        bc = bc.transpose(1, 0, 3, 2).reshape(g_n, tt * c_n, p_n)
        re, im = jnp.real(bc), jnp.imag(bc)
        return jnp.concatenate([re, im, im, re], axis=-1)

    b_in = jnp.concatenate([lay_in(bin_f), lay_in(bin_b)], axis=-1)
    d_f = c_mat[None] * powers[1 + jnp.arange(tt), 0][:, :, None, :]
    d_b = c_mat[None] * powers[tt - jnp.arange(tt), 1][:, :, None, :]

    def lay_out(dc):
        dc = dc.transpose(1, 3, 0, 2).reshape(g_n, p_n, tt * c_n)
        return jnp.concatenate([jnp.real(dc), -jnp.imag(dc)], axis=1)

    c_out = jnp.concatenate([lay_out(d_f), lay_out(d_b)], axis=1)
    lt = powers[tt]
    lr, li = jnp.real(lt), jnp.imag(lt)
    lam_t = jnp.stack(
        [
            jnp.concatenate([lr[0], lr[0]], -1),
            jnp.concatenate([-li[0], li[0]], -1),
            jnp.concatenate([lr[1], lr[1]], -1),
            jnp.concatenate([-li[1], li[1]], -1),
        ],
        axis=1,
    )
    return toep.astype(BF16), b_in.astype(BF16), c_out.astype(BF16), lam_t


def _s5_kernel(bsz, u_ref, toep_ref, bin_ref, cout_ref, lam_ref, s0_ref, y_ref, s1_ref, v_s, st_s):
    u = u_ref[0].astype(BF16)
    v_s[...] = jnp.dot(u, bin_ref[0], preferred_element_type=F32)
    nk = u.shape[0] // bsz
    p2 = lam_ref.shape[2]
    lam = lam_ref[0]
    a_f, b_f, a_b, b_b = lam[0:1], lam[1:2], lam[2:3], lam[3:4]
    s0 = s0_ref[0]

    def body(k, carry):
        sf, sfx, sb, sbx = carry
        rf = pl.multiple_of(k * bsz, bsz)
        rb = pl.multiple_of((nk - 1 - k) * bsz, bsz)
        st_s[pl.ds(rf, bsz), 0:p2] = sf
        st_s[pl.ds(rb, bsz), p2 : 2 * p2] = sb
        vf = v_s[pl.ds(rf, bsz), 0 : 2 * p2]
        vb = v_s[pl.ds(rb, bsz), 2 * p2 : 4 * p2]
        sf_n = a_f * sf + b_f * sfx + vf[:, 0:p2]
        sfx_n = a_f * sfx - b_f * sf + vf[:, p2 : 2 * p2]
        sb_n = a_b * sb + b_b * sbx + vb[:, 0:p2]
        sbx_n = a_b * sbx - b_b * sb + vb[:, p2 : 2 * p2]
        return sf_n, sfx_n, sb_n, sbx_n

    carry = (s0[:, 0:p2], s0[:, p2 : 2 * p2], s0[:, 2 * p2 : 3 * p2], s0[:, 3 * p2 : 4 * p2])
    sf, sfx, sb, sbx = lax.fori_loop(0, nk, body, carry)
    s1_ref[0] = jnp.concatenate([sf, sfx, sb, sbx], axis=-1)
    y = jnp.dot(u, toep_ref[0], preferred_element_type=F32)
    y += jnp.dot(st_s[...].astype(BF16), cout_ref[0], preferred_element_type=F32)
    y_ref[0] = y


def s5_bidir(u_rows, mats, s0):
    toep, b_in, c_out, lam_t = mats
    g_n, rows, width = u_rows.shape
    bsz = s0.shape[1]
    p8 = b_in.shape[2]
    p2 = p8 // 4
    return pl.pallas_call(
        functools.partial(_s5_kernel, bsz),
        grid=(g_n,),
        in_specs=[
            pl.BlockSpec((1, rows, width), lambda g: (g, 0, 0)),
            pl.BlockSpec((1, width, width), lambda g: (g, 0, 0)),
            pl.BlockSpec((1, width, p8), lambda g: (g, 0, 0)),
            pl.BlockSpec((1, 2 * p2, width), lambda g: (g, 0, 0)),
            pl.BlockSpec((1, 4, p2), lambda g: (g, 0, 0)),
            pl.BlockSpec((1, bsz, p8), lambda g: (g, 0, 0)),
        ],
        out_specs=[
            pl.BlockSpec((1, rows, width), lambda g: (g, 0, 0)),
            pl.BlockSpec((1, bsz, p8), lambda g: (g, 0, 0)),
        ],
        out_shape=[jax.ShapeDtypeStruct(u_rows.shape, F32), jax.ShapeDtypeStruct(s0.shape, F32)],
        scratch_shapes=[pltpu.VMEM((rows, p8), F32), pltpu.VMEM((rows, 2 * p2), F32)],
        compiler_params=_cparams("arbitrary"),
    )(u_rows, toep, b_in, c_out, lam_t, s0)


def _s5_glu_kernel(ys_ref, u_ref, d_ref, w_ref, o_ref):
    y = ys_ref[0] + d_ref[...] * u_ref[0]
    y = jax.nn.gelu(y)
    gate = jax.nn.sigmoid(jnp.dot(y.astype(BF16), w_ref[...], preferred_element_type=F32))
    o_ref[0] = (y * gate).astype(o_ref.dtype)


def s5_glu(ys, z, u_col_block, d_skip, w_glu):
    bsz, seq, sd = ys.shape
    tl = _tile(seq, 512)
    return pl.pallas_call(
        _s5_glu_kernel,
        grid=(bsz, seq // tl),
        in_specs=[
            pl.BlockSpec((1, tl, sd), lambda b, i: (b, i, 0)),
            pl.BlockSpec((1, tl, sd), lambda b, i: (b, i, u_col_block)),
            pl.BlockSpec((1, sd), lambda b, i: (0, 0)),
            pl.BlockSpec((sd, sd), lambda b, i: (0, 0)),
        ],
        out_specs=pl.BlockSpec((1, tl, sd), lambda b, i: (b, i, 0)),
        out_shape=jax.ShapeDtypeStruct((bsz, seq, sd), BF16),
        compiler_params=_cparams("arbitrary", "arbitrary"),
    )(ys, z, d_skip.reshape(1, sd), w_glu)


def _to_s5_rows(z, u_col0, sd):
    bsz, seq = z.shape[:2]
    g_n = sd // S5_GROUP
    u = z[:, :, u_col0 : u_col0 + sd].reshape(bsz, seq // S5_TILE, S5_TILE, g_n, S5_GROUP)
    return u.transpose(3, 1, 0, 2, 4).reshape(g_n, (seq // S5_TILE) * bsz, S5_TILE * S5_GROUP)


def _from_s5_rows(y, bsz):
    g_n, rows, width = y.shape
    nk = rows // bsz
    y = y.reshape(g_n, nk, bsz, S5_TILE, S5_GROUP).transpose(2, 1, 3, 0, 4)
    return y.reshape(bsz, nk * S5_TILE, g_n * S5_GROUP)


def _hybrid_prep(h, w_main, w_gates, conv_w, col_scale, md):
    bsz, seq, d = h.shape
    z = matmul([(h.reshape(bsz * seq, d), w_main)], F32).reshape(bsz, seq, -1)
    gates = matmul([(h.reshape(bsz * seq, d), w_gates)], F32, tn=LANES).reshape(bsz, seq, LANES)
    qk = qk_conv(z, conv_w, col_scale, 2 * md)
    return z, gates, qk


def hybrid_mixer(h_c, h_l, p, need_ctx):
    d = h_l.shape[2]
    md = p["head_norm"].shape[0]
    sd = d - md
    nh = MLSTM_HEADS
    dh = md // nh
    bsz = h_l.shape[0]
    g_n = sd // S5_GROUP
    v_blk, o_blk, u_blk = 2, 3, 4
    outs = []
    m_state = (
        jnp.zeros((bsz, 2 * nh, dh, dh), F32),
        jnp.zeros((bsz, 2 * nh, 8, dh), F32),
        jnp.zeros((bsz, 2 * nh, 8, LANES), F32),
    )
    s_state = jnp.zeros((g_n, bsz, 8 * p["s5_mats"][3].shape[2] // 2), F32)
    for h, want_out in ((h_c, need_ctx), (h_l, True)):
        z, gates, qk = _hybrid_prep(h, p["w_main"], p["w_gates"], p["conv"], p["col_scale"], md)
        seq = h.shape[1]
        h_f, h_b, m_state = mlstm_bidir(
            qk, jnp.swapaxes(qk, 1, 2), z, v_blk, gates, jnp.swapaxes(gates[:, :, : 4 * nh], 1, 2), p["b_gates"], m_state
        )
        ys_rows, s_state = s5_bidir(_to_s5_rows(z, u_blk * sd, sd), p["s5_mats"], s_state)
        if not want_out:
            outs.append(None)
            continue
        hm = mlstm_finish(h_f, h_b, z, o_blk, p["head_norm"])
        y = s5_glu(_from_s5_rows(ys_rows, bsz), z, u_blk, p["d_skip"], p["w_glu"])
        outs.append((hm.reshape(bsz * seq, md), y.reshape(bsz * seq, sd)))
    return outs


def kernel(x, c, ctx, c_ctx, ada_w, ada_b, norm_mix, norm_ffn, norm_final, hy_w_in, hy_b_gates, hy_conv, hy_head_norm, s5_lam_re, s5_lam_im, s5_log_step, s5_b_re, s5_b_im, s5_c_re, s5_c_im, s5_d, s5_w_glu, hy_w_out, da_w_qkv, da_lambda, da_head_norm, da_w_o, ec_router, ec_w_gate, ec_w_up, ec_w_down):
    bsz, seq, d = x.shape
    lc = ctx.shape[1]
    depth = ada_w.shape[0]
    md = hy_head_norm.shape[1]
    sd = d - md
    nh = MLSTM_HEADS
    hd = d // (2 * DIFF_HEADS)

    rows = -(-(bsz + 1) // 8) * 8
    cond = jnp.zeros((rows, d), F32).at[:bsz].set(c).at[bsz].set(c_ctx)
    mod = ada_modulation(cond, ada_w, ada_b)
    mod_l = mod[:, :bsz].reshape(depth, bsz, ADA_CHUNKS, 1, d)
    mod_c = jnp.broadcast_to(mod[:, bsz : bsz + 1].reshape(depth, 1, ADA_CHUNKS, 1, d), (depth, bsz, ADA_CHUNKS, 1, d))

    n_rows = seq // GRID_W
    row_pos = jnp.repeat(jnp.arange(n_rows, dtype=F32), GRID_W)
    col_pos = jnp.tile(jnp.arange(GRID_W, dtype=F32), n_rows)
    axis_dim = hd // 2
    inv_freq = ROPE_BASE ** (-jnp.arange(0, axis_dim, 2, dtype=F32) / axis_dim)
    ar = row_pos[:, None] * inv_freq[None]
    ac = col_pos[:, None] * inv_freq[None]
    ang = jnp.concatenate([ar, ar, ac, ac], axis=-1)
    quarter = hd // 4
    sign = jnp.where((jnp.arange(hd) % (2 * quarter)) < quarter, -1.0, 1.0)
    cos_t, sin_t = jnp.cos(ang), jnp.sin(ang) * sign

    for layer in range(depth):
        need_ctx = layer < depth - 1
        j = layer // 2
        ml = [mod_l[layer, :, i] for i in range(ADA_CHUNKS)]
        mc = [mod_c[layer, :, i] for i in range(ADA_CHUNKS)]
        (h_l,) = norm_modulate(x, norm_mix[layer], ml[1], ml[0], [BF16])
        (h_c,) = norm_modulate(ctx, norm_mix[layer], mc[1], mc[0], [BF16])
        if layer % 2 == 0:
            w_in = hy_w_in[j]
            g0 = 4 * md
            g1 = g0 + 4 * nh
            w_main = jnp.concatenate([w_in[:, :g0], w_in[:, g1:]], axis=1).astype(BF16)
            w_gates = jnp.zeros((d, LANES), F32).at[:, : 4 * nh].set(w_in[:, g0:g1]).astype(BF16)
            col_scale = jnp.concatenate([jnp.ones((md,), F32), jnp.full((md,), (md // nh) ** -0.5, F32)]).reshape(1, 2 * md)
            p = dict(
                w_main=w_main, w_gates=w_gates, conv=hy_conv[j], col_scale=col_scale, b_gates=hy_b_gates[j],
                head_norm=hy_head_norm[j], d_skip=s5_d[j], w_glu=s5_w_glu[j].astype(BF16),
                s5_mats=s5_matrices(s5_lam_re[j], s5_lam_im[j], s5_log_step[j], s5_b_re[j], s5_b_im[j], s5_c_re[j], s5_c_im[j]),
            )
            mix_c, mix_l = hybrid_mixer(h_c, h_l, p, need_ctx)
            w_out = hy_w_out[j].astype(BF16)
            pairs_l = [(mix_l[0], w_out[:md]), (mix_l[1], w_out[md:])]
            pairs_c = [(mix_c[0], w_out[:md]), (mix_c[1], w_out[md:])] if need_ctx else None
        else:
            lam_init = 0.8 - 0.6 * math.exp(-0.3 * layer)
            lv = da_lambda[j].astype(F32)
            lam = jnp.exp(jnp.sum(lv[0] * lv[1])) - jnp.exp(jnp.sum(lv[2] * lv[3])) + lam_init
            w_qkv = da_w_qkv[j].astype(BF16)
            q_scale = jnp.concatenate([jnp.full((d,), hd**-0.5 * math.log2(math.e), F32), jnp.ones((2 * d,), F32)]).reshape(1, 3 * d)
            qkv_l = matmul([(h_l.reshape(bsz * seq, d), w_qkv)], BF16, rope=(cos_t, sin_t, 2 * d, seq), col_scale=q_scale).reshape(bsz, seq, 3 * d)
            qkv_c = matmul([(h_c.reshape(bsz * lc, d), w_qkv)], BF16, col_scale=q_scale).reshape(bsz, lc, 3 * d)
            o_l = diff_attention(qkv_l, [qkv_c, qkv_l], lam, da_head_norm[j], 1.0 - lam_init)
            w_o = da_w_o[j].astype(BF16)
            pairs_l = [(o_l.reshape(bsz * seq, d), w_o)]
            pairs_c = None
            if need_ctx:
                o_c = diff_attention(qkv_c, [qkv_c], lam, da_head_norm[j], 1.0 - lam_init)
                pairs_c = [(o_c.reshape(bsz * lc, d), w_o)]
        x = matmul(pairs_l, F32, residual=x.reshape(bsz * seq, d), gate=ml[2], rows_per_batch=seq).reshape(bsz, seq, d)
        w_rt = ec_router[layer].T.astype(BF16)
        w_g, w_u, w_d = (w[layer].astype(BF16) for w in (ec_w_gate, ec_w_up, ec_w_down))
        hb, hf = norm_modulate(x, norm_ffn[layer], ml[4], ml[3], [BF16, F32])
        x = expert_choice_ffn(x, hb, hf, ml[5], w_rt, w_g, w_u, w_d)
        if need_ctx:
            ctx = matmul(pairs_c, F32, residual=ctx.reshape(bsz * lc, d), gate=mc[2], rows_per_batch=lc).reshape(bsz, lc, d)
            hb, hf = norm_modulate(ctx, norm_ffn[layer], mc[4], mc[3], [BF16, F32])
            ctx = expert_choice_ffn(ctx, hb, hf, mc[5], w_rt, w_g, w_u, w_d)
    return final_norm(x, norm_final)
```

```python
import functools
import math

import jax
import jax.numpy as jnp
from jax import lax
from jax.experimental import pallas as pl
from jax.experimental.pallas import tpu as pltpu

F32 = jnp.float32
BF16 = jnp.bfloat16

GRID_W = 64
EPS = 1e-6
ADA_CHUNKS = 6
MLSTM_HEADS = 4
QK_CONV = 3
S5_GROUP = 16
DIFF_HEADS = 8
ROPE_BASE = 10000.0
CAPACITY_FACTOR = 2

MLSTM_TILE = 256
S5_TILE = 16
VMEM_LIMIT_BYTES = 56 * 1024 * 1024
ROW_TILE = 1024
COL_TILE = 1024
ATTN_Q_TILE = 1024
ATTN_SUB_TILE = 256
ATTN_SCORE_SLOTS = 3
ATTN_KEY_CHUNK = 512
LANES = 128
SUBLANES = 8


def _cparams(*sem):
    return pltpu.CompilerParams(dimension_semantics=sem, vmem_limit_bytes=VMEM_LIMIT_BYTES)


def _tile(n, pref):
    t = min(n, pref)
    while n % t:
        t //= 2
    return t


def _ada_kernel(c_ref, w_ref, b_ref, o_ref):
    cv = c_ref[...]
    a = (cv * jax.nn.sigmoid(cv)).astype(BF16)
    o_ref[0] = jnp.dot(a, w_ref[0].astype(BF16), preferred_element_type=F32) + b_ref[0]


def ada_modulation(cond, ada_w, ada_b):
    depth, d, n = ada_w.shape
    rows = cond.shape[0]
    tn = _tile(n, COL_TILE)
    return pl.pallas_call(
        _ada_kernel,
        grid=(depth, n // tn),
        in_specs=[
            pl.BlockSpec((rows, d), lambda l, j: (0, 0)),
            pl.BlockSpec((1, d, tn), lambda l, j: (l, 0, j)),
            pl.BlockSpec((1, 1, tn), lambda l, j: (l, 0, j)),
        ],
        out_specs=pl.BlockSpec((1, rows, tn), lambda l, j: (l, 0, j)),
        out_shape=jax.ShapeDtypeStruct((depth, rows, n), F32),
        compiler_params=_cparams("arbitrary", "arbitrary"),
    )(cond, ada_w, ada_b.reshape(depth, 1, n))


def _normmod_kernel(x_ref, g_ref, sc_ref, sh_ref, *o_refs):
    x = x_ref[0]
    y = x * lax.rsqrt(jnp.mean(x * x, axis=-1, keepdims=True) + EPS) * g_ref[...]
    y = y * (1.0 + sc_ref[0]) + sh_ref[0]
    for o_ref in o_refs:
        o_ref[0] = y.astype(o_ref.dtype)


def norm_modulate(x, g, scale, shift, dtypes):
    bsz, seq, d = x.shape
    tl = _tile(seq, 512)
    return pl.pallas_call(
        _normmod_kernel,
        grid=(bsz, seq // tl),
        in_specs=[
            pl.BlockSpec((1, tl, d), lambda b, i: (b, i, 0)),
            pl.BlockSpec((1, d), lambda b, i: (0, 0)),
            pl.BlockSpec((1, 1, d), lambda b, i: (b, 0, 0)),
            pl.BlockSpec((1, 1, d), lambda b, i: (b, 0, 0)),
        ],
        out_specs=[pl.BlockSpec((1, tl, d), lambda b, i: (b, i, 0)) for _ in dtypes],
        out_shape=[jax.ShapeDtypeStruct(x.shape, dt) for dt in dtypes],
        compiler_params=_cparams("arbitrary", "arbitrary"),
    )(x, g.reshape(1, d), scale, shift)


def _final_norm_kernel(x_ref, g_ref, o_ref):
    x = x_ref[0]
    o_ref[0] = x * lax.rsqrt(jnp.mean(x * x, axis=-1, keepdims=True) + EPS) * g_ref[...]


def final_norm(x, g):
    bsz, seq, d = x.shape
    tl = _tile(seq, 512)
    return pl.pallas_call(
        _final_norm_kernel,
        grid=(bsz, seq // tl),
        in_specs=[pl.BlockSpec((1, tl, d), lambda b, i: (b, i, 0)), pl.BlockSpec((1, d), lambda b, i: (0, 0))],
        out_specs=pl.BlockSpec((1, tl, d), lambda b, i: (b, i, 0)),
        out_shape=jax.ShapeDtypeStruct(x.shape, F32),
        compiler_params=_cparams("arbitrary", "arbitrary"),
    )(x, g.reshape(1, d))


def _mm_kernel(n_pairs, *refs):
    o_ref = refs[2 * n_pairs]
    acc = jnp.dot(refs[0][...], refs[1][...], preferred_element_type=F32)
    for p in range(1, n_pairs):
        acc += jnp.dot(refs[2 * p][...], refs[2 * p + 1][...], preferred_element_type=F32)
    o_ref[...] = acc.astype(o_ref.dtype)


def _mm_colscale_kernel(a_ref, w_ref, cs_ref, o_ref):
    acc = jnp.dot(a_ref[...], w_ref[...], preferred_element_type=F32)
    o_ref[...] = (acc * cs_ref[...]).astype(o_ref.dtype)


def _mm_residual_kernel(n_pairs, *refs):
    r_ref, g_ref, o_ref = refs[2 * n_pairs : 2 * n_pairs + 3]
    acc = jnp.dot(refs[0][...], refs[1][...], preferred_element_type=F32)
    for p in range(1, n_pairs):
        acc += jnp.dot(refs[2 * p][...], refs[2 * p + 1][...], preferred_element_type=F32)
    o_ref[...] = r_ref[...] + g_ref[0] * acc


def _mm_rope_kernel(n_rope_tiles, quarter, a_ref, w_ref, cs_ref, cos_ref, sin_ref, o_ref):
    acc = jnp.dot(a_ref[...], w_ref[...], preferred_element_type=F32) * cs_ref[...]
    j = pl.program_id(1)

    @pl.when(j >= n_rope_tiles)
    def _():
        o_ref[...] = acc.astype(o_ref.dtype)

    @pl.when(j < n_rope_tiles)
    def _():
        cos = cos_ref[...]
        sin = sin_ref[...]
        hd = cos.shape[1]
        lane = lax.broadcasted_iota(jnp.int32, (acc.shape[0], hd), 1)
        first = (lane % (2 * quarter)) < quarter
        for g in range(acc.shape[1] // hd):
            xg = acc[:, g * hd : (g + 1) * hd]
            up = pltpu.roll(xg, hd - quarter, 1)
            dn = pltpu.roll(xg, quarter, 1)
            o_ref[:, g * hd : (g + 1) * hd] = (xg * cos + jnp.where(first, up, dn) * sin).astype(o_ref.dtype)


def matmul(pairs, out_dtype, *, tm=ROW_TILE, tn=COL_TILE, residual=None, gate=None, rows_per_batch=None, rope=None,
           col_scale=None):
    m = pairs[0][0].shape[0]
    n = pairs[0][1].shape[1]
    if rope is not None:
        rows_per_batch = rope[3]
    tm = _tile(m if rows_per_batch is None else rows_per_batch, tm)
    tn = _tile(n, tn)
    grid = (m // tm, n // tn)
    in_specs, args = [], []
    for a, w in pairs:
        k = a.shape[1]
        in_specs += [pl.BlockSpec((tm, k), lambda i, j: (i, 0)), pl.BlockSpec((k, tn), lambda i, j: (0, j))]
        args += [a, w]
    out_spec = pl.BlockSpec((tm, tn), lambda i, j: (i, j))
    aliases = {}
    if residual is not None:
        bpt = rows_per_batch // tm
        in_specs += [pl.BlockSpec((tm, tn), lambda i, j: (i, j)), pl.BlockSpec((1, 1, tn), lambda i, j: (i // bpt, 0, j))]
        args += [residual, gate]
        aliases = {2 * len(pairs): 0}
        body = functools.partial(_mm_residual_kernel, len(pairs))
    elif rope is not None:
        cos, sin, n_rope_cols, rpb = rope
        hd = cos.shape[1]
        bpt = rpb // tm
        in_specs += [
            pl.BlockSpec((1, tn), lambda i, j: (0, j)),
            pl.BlockSpec((tm, hd), lambda i, j: (i % bpt, 0)),
            pl.BlockSpec((tm, hd), lambda i, j: (i % bpt, 0)),
        ]
        args += [col_scale, cos, sin]
        body = functools.partial(_mm_rope_kernel, n_rope_cols // tn, hd // 4)
    elif col_scale is not None:
        in_specs.append(pl.BlockSpec((1, tn), lambda i, j: (0, j)))
        args.append(col_scale)
        body = _mm_colscale_kernel
    else:
        body = functools.partial(_mm_kernel, len(pairs))
    return pl.pallas_call(
        body,
        grid=grid,
        in_specs=in_specs,
        out_specs=out_spec,
        out_shape=jax.ShapeDtypeStruct((m, n), out_dtype),
        input_output_aliases=aliases,
        compiler_params=_cparams("arbitrary", "arbitrary"),
    )(*args)


def _attn_kernel(n_seg, seg_off, hd, sub, out_scale, lam_ref, q_ref, *refs):
    kv_refs = refs[: 2 * n_seg]
    g_ref, o_ref, s_scr, p_scr = refs[2 * n_seg : 2 * n_seg + 4]
    lam = lam_ref[0]
    nt = (((1,), (1,)), ((), ()))
    n_sub = q_ref.shape[1] // sub
    n_chain = 2 * n_sub
    n_s, n_p = s_scr.shape[0], p_scr.shape[0]
    row_max, row_sum, outs = {}, {}, {}
    max_w = math.gcd(LANES, *[seg_off[s + 1] - seg_off[s] for s in range(n_seg)])

    def stage_scores(c):
        r, i = divmod(c, 2)
        qi = q_ref[0, r * sub : (r + 1) * sub, i * hd : (i + 1) * hd]
        m_lanes = None
        for s in range(n_seg):
            seg_len = seg_off[s + 1] - seg_off[s]
            step = min(ATTN_KEY_CHUNK, seg_len)
            for k0 in range(0, seg_len, step):
                kc = kv_refs[2 * s][0, k0 : k0 + step, i * hd : (i + 1) * hd]
                sc = lax.dot_general(qi, kc, nt, preferred_element_type=F32)
                s_scr[c % n_s, :, seg_off[s] + k0 : seg_off[s] + k0 + step] = sc
                for l0 in range(0, step, max_w):
                    piece = sc[:, l0 : l0 + max_w]
                    m_lanes = piece if m_lanes is None else jnp.maximum(m_lanes, piece)
        row_max[c] = m_lanes.max(axis=-1, keepdims=True)

    def stage_probs(c):
        p = jnp.exp2(s_scr[c % n_s] - row_max[c])
        row_sum[c] = p.sum(axis=-1, keepdims=True)
        p_scr[c % n_p] = p.astype(BF16)

    def stage_values(c):
        acc = None
        for s in range(n_seg):
            lo, hi = seg_off[s], seg_off[s + 1]
            part = jnp.dot(p_scr[c % n_p, :, lo:hi], kv_refs[2 * s + 1][0], preferred_element_type=F32)
            acc = part if acc is None else acc + part
        outs[c] = acc / row_sum[c]
        if c % 2 == 1:
            r = c // 2
            o = outs.pop(c - 1) - lam * outs.pop(c)
            o = o * lax.rsqrt(jnp.mean(o * o, axis=-1, keepdims=True) + EPS) * g_ref[...] * out_scale
            o_ref[0, r * sub : (r + 1) * sub, :] = o.astype(o_ref.dtype)

    for t in range(n_chain + 2):
        if t < n_chain:
            stage_scores(t)
        if 0 <= t - 1 < n_chain:
            stage_probs(t - 1)
        if 0 <= t - 2 < n_chain:
            stage_values(t - 2)


def diff_attention(q_src, kv_srcs, lam, head_norm, out_scale):
    bsz, lq, d3 = q_src.shape
    d = d3 // 3
    hw = d // DIFF_HEADS
    hd = hw // 2
    tq = _tile(lq, ATTN_Q_TILE)
    sub = _tile(tq, ATTN_SUB_TILE)
    seg_off = [0]
    for src in kv_srcs:
        seg_off.append(seg_off[-1] + src.shape[1])
    in_specs = [
        pl.BlockSpec(memory_space=pltpu.SMEM),
        pl.BlockSpec((1, tq, hw), lambda b, h, i: (b, i, h)),
    ]
    args = [lam.reshape(1), q_src]
    for src in kv_srcs:
        ls = src.shape[1]
        in_specs += [
            pl.BlockSpec((1, ls, hw), lambda b, h, i: (b, 0, DIFF_HEADS + h)),
            pl.BlockSpec((1, ls, hw), lambda b, h, i: (b, 0, 2 * DIFF_HEADS + h)),
        ]
        args += [src, src]
    in_specs.append(pl.BlockSpec((1, hw), lambda b, h, i: (0, h)))
    args.append(head_norm.reshape(1, d))
    return pl.pallas_call(
        functools.partial(_attn_kernel, len(kv_srcs), tuple(seg_off), hd, sub, out_scale),
        grid=(bsz, DIFF_HEADS, lq // tq),
        in_specs=in_specs,
        out_specs=pl.BlockSpec((1, tq, hw), lambda b, h, i: (b, i, h)),
        out_shape=jax.ShapeDtypeStruct((bsz, lq, d), BF16),
        scratch_shapes=[
            pltpu.VMEM((ATTN_SCORE_SLOTS, sub, seg_off[-1]), F32),
            pltpu.VMEM((ATTN_SCORE_SLOTS, sub, seg_off[-1]), BF16),
        ],
        compiler_params=_cparams("arbitrary", "arbitrary", "arbitrary"),
    )(*args)


def _normmod_router_kernel(x_ref, g_ref, sc_ref, sh_ref, w_ref, h_ref, aff_ref):
    x = x_ref[0]
    y = x * lax.rsqrt(jnp.mean(x * x, axis=-1, keepdims=True) + EPS) * g_ref[...]
    y = y * (1.0 + sc_ref[0]) + sh_ref[0]
    h_ref[0] = y
    nt = (((1,), (1,)), ((), ()))
    logits = lax.dot_general(w_ref[...], y.astype(BF16), nt, preferred_element_type=F32)
    z = jnp.exp(logits - logits.max(axis=0, keepdims=True))
    aff_ref[0] = z / z.sum(axis=0, keepdims=True)


def norm_modulate_router(x, g, scale, shift, w_router_t):
    bsz, seq, d = x.shape
    ne = w_router_t.shape[0]
    tl = _tile(seq, 512)
    return pl.pallas_call(
        _normmod_router_kernel,
        grid=(bsz, seq // tl),
        in_specs=[
            pl.BlockSpec((1, tl, d), lambda b, i: (b, i, 0)),
            pl.BlockSpec((1, d), lambda b, i: (0, 0)),
            pl.BlockSpec((1, 1, d), lambda b, i: (b, 0, 0)),
            pl.BlockSpec((1, 1, d), lambda b, i: (b, 0, 0)),
            pl.BlockSpec((ne, d), lambda b, i: (0, 0)),
        ],
        out_specs=[pl.BlockSpec((1, tl, d), lambda b, i: (b, i, 0)), pl.BlockSpec((1, ne, tl), lambda b, i: (b, 0, i))],
        out_shape=[jax.ShapeDtypeStruct(x.shape, F32), jax.ShapeDtypeStruct((bsz, ne, seq), F32)],
        compiler_params=_cparams("arbitrary", "arbitrary"),
    )(x, g.reshape(1, d), scale, shift, w_router_t)


def _moe_kernel(cap, nb, idx_ref, idx_next_ref, gate_ref, mod_ref, h_hbm, x_in, wg_ref, wu_ref, wd_ref, x_out,
                xs_buf, xr_buf, sems):
    del x_in
    e, b = pl.program_id(0), pl.program_id(1)
    n = e * nb + b
    n_last = pl.num_programs(0) * nb - 1
    slot = n % 2
    other = 1 - slot
    sem_h, sem_in, sem_out = 0, 1, 2

    def start_rows(rows_of, idx, buf_slot, sem, to_hbm=False):
        def body(g, carry):
            for k in range(SUBLANES):
                hbm_row = rows_of.at[pl.ds(idx[0, 0, g * SUBLANES + k], 1)]
                buf_row = buf_slot.at[g, pl.ds(k, 1)]
                if to_hbm:
                    pltpu.make_async_copy(buf_row, hbm_row, sem).start()
                else:
                    pltpu.make_async_copy(hbm_row, buf_row, sem).start()
            return carry

        lax.fori_loop(0, cap // SUBLANES, body, 0)

    def wait_all(buf_slot, sem):
        pltpu.make_async_copy(buf_slot, buf_slot, sem).wait()

    @pl.when(n == 0)
    def _():
        start_rows(h_hbm.at[b], idx_ref, xs_buf.at[slot], sems.at[sem_h, slot])

    @pl.when(n < n_last)
    def _():
        start_rows(h_hbm.at[(b + 1) % nb], idx_next_ref, xs_buf.at[other], sems.at[sem_h, other])

    start_rows(x_out.at[b], idx_ref, xr_buf.at[slot], sems.at[sem_in, slot])

    wait_all(xs_buf.at[slot], sems.at[sem_h, slot])
    d = xs_buf.shape[-1]
    xs = xs_buf[slot].reshape(cap, d).astype(BF16)
    gt = jnp.dot(xs, wg_ref[0], preferred_element_type=F32)
    up = jnp.dot(xs, wu_ref[0], preferred_element_type=F32)
    hid = (gt * jax.nn.sigmoid(gt) * up).astype(BF16)
    out = jnp.dot(hid, wd_ref[0], preferred_element_type=F32) * gate_ref[0] * mod_ref[0]
    wait_all(xr_buf.at[slot], sems.at[sem_in, slot])
    xr_buf[slot] += out.reshape(cap // SUBLANES, SUBLANES, d)

    @pl.when(n > 0)
    def _():
        wait_all(xr_buf.at[other], sems.at[sem_out, other])

    start_rows(x_out.at[b], idx_ref, xr_buf.at[slot], sems.at[sem_out, slot], to_hbm=True)

    @pl.when(n == n_last)
    def _():
        wait_all(xr_buf.at[slot], sems.at[sem_out, slot])


def moe_apply(x, h, idx, gate, mod, w_gate, w_up, w_down):
    bsz, seq, d = x.shape
    ne, cap = idx.shape[1], idx.shape[2]
    ff = w_gate.shape[2]
    idx_rows = idx.reshape(bsz * ne, 1, cap)

    def next_rows(e, b):
        nxt = jnp.minimum(e * bsz + b + 1, ne * bsz - 1)
        return ((nxt % bsz) * ne + nxt // bsz, 0, 0)

    single = pl.Buffered(1)
    return pl.pallas_call(
        functools.partial(_moe_kernel, cap, bsz),
        grid=(ne, bsz),
        in_specs=[
            pl.BlockSpec((1, 1, cap), lambda e, b: (b * ne + e, 0, 0), memory_space=pltpu.SMEM),
            pl.BlockSpec((1, 1, cap), next_rows, memory_space=pltpu.SMEM),
            pl.BlockSpec((1, cap, 1), lambda e, b: (b * ne + e, 0, 0)),
            pl.BlockSpec((1, 1, d), lambda e, b: (b, 0, 0)),
            pl.BlockSpec(memory_space=pl.ANY),
            pl.BlockSpec(memory_space=pl.ANY),
            pl.BlockSpec((1, d, ff), lambda e, b: (e, 0, 0), pipeline_mode=single),
            pl.BlockSpec((1, d, ff), lambda e, b: (e, 0, 0), pipeline_mode=single),
            pl.BlockSpec((1, ff, d), lambda e, b: (e, 0, 0), pipeline_mode=single),
        ],
        out_specs=pl.BlockSpec(memory_space=pl.ANY),
        out_shape=jax.ShapeDtypeStruct(x.shape, F32),
        scratch_shapes=[
            pltpu.VMEM((2, cap // SUBLANES, SUBLANES, d), F32),
            pltpu.VMEM((2, cap // SUBLANES, SUBLANES, d), F32),
            pltpu.SemaphoreType.DMA((3, 2)),
        ],
        input_output_aliases={5: 0},
        compiler_params=_cparams("arbitrary", "arbitrary"),
    )(idx_rows, idx_rows, gate.reshape(bsz * ne, cap, 1), mod, h, x, w_gate, w_up, w_down)


def expert_choice_ffn(x, norm_g, scale, shift, mod, w_router_t, w_gate, w_up, w_down):
    seq = x.shape[1]
    ne = w_router_t.shape[0]
    cap = CAPACITY_FACTOR * seq // ne
    h, aff = norm_modulate_router(x, norm_g, scale, shift, w_router_t)
    gate, idx = lax.top_k(aff, cap)
    return moe_apply(x, h, idx.astype(jnp.int32), gate, mod, w_gate, w_up, w_down)


def _conv_kernel(z_ref, w_ref, s_ref, o_ref):
    x = z_ref[0]
    seq = x.shape[0]
    row = lax.broadcasted_iota(jnp.int32, x.shape, 0)
    prev = jnp.where(row == 0, 0.0, pltpu.roll(x, 1, 0))
    nxt = jnp.where(row == seq - 1, 0.0, pltpu.roll(x, seq - 1, 0))
    w = w_ref[...]
    y = prev * w[0:1] + x * w[1:2] + nxt * w[2:3]
    o_ref[0] = (y * jax.nn.sigmoid(y) * s_ref[...]).astype(o_ref.dtype)


def qk_conv(z, conv_w, col_scale, n_cols):
    bsz, seq = z.shape[:2]
    tc = 256
    return pl.pallas_call(
        _conv_kernel,
        grid=(bsz, n_cols // tc),
        in_specs=[
            pl.BlockSpec((1, seq, tc), lambda b, j: (b, 0, j)),
            pl.BlockSpec((QK_CONV, tc), lambda b, j: (0, j)),
            pl.BlockSpec((1, tc), lambda b, j: (0, j)),
        ],
        out_specs=pl.BlockSpec((1, seq, tc), lambda b, j: (b, 0, j)),
        out_shape=jax.ShapeDtypeStruct((bsz, seq, n_cols), BF16),
        compiler_params=_cparams("arbitrary", "arbitrary"),
    )(z, conv_w, col_scale)


def _log_sigmoid(x):
    return jnp.minimum(x, 0.0) - jnp.log(1.0 + jnp.exp(-jnp.abs(x)))


def _mlstm_kernel(nh, dh, *refs):
    dir_refs = (refs[0:6], refs[6:12])
    bias_ref, bias_t_ref, c0_ref, n0_ref, m0_ref = refs[12:17]
    h_outs = refs[17:19]
    c_out, n_out, m_out = refs[19:22]
    c_s, n_s, m_s = refs[22:25]
    kc = pl.program_id(1)
    last = pl.num_programs(1) - 1

    @pl.when(kc == 0)
    def _():
        c_s[...] = c0_ref[0]
        n_s[...] = n0_ref[0]
        m_s[...] = m0_ref[0]

    tt = dir_refs[0][0].shape[1]
    row = lax.broadcasted_iota(jnp.int32, (tt, tt), 0)
    col = lax.broadcasted_iota(jnp.int32, (tt, tt), 1)
    nt = (((1,), (1,)), ((), ()))
    hi = lax.Precision.HIGHEST
    for d in range(2):
        q_ref, k_ref, kt_ref, v_ref, g_ref, gt_ref = dir_refs[d]
        window = (col <= row) if d == 0 else (col >= row)
        tri = window.astype(F32)
        gates = g_ref[0] + bias_ref[...]
        gates_t = gt_ref[0] + bias_t_ref[...]
        b_cols = jnp.dot(tri, _log_sigmoid(gates), precision=hi, preferred_element_type=F32)
        b_rows = lax.dot_general(_log_sigmoid(gates_t), tri, nt, precision=hi, preferred_element_type=F32)
        edge = tt - 1 if d == 0 else 0
        for h in range(nh):
            ci = 2 * nh * d + h
            cf = ci + nh
            sl = slice(h * dh, (h + 1) * dh)
            st = d * nh + h
            i_col, i_row = gates[:, ci : ci + 1], gates_t[ci : ci + 1, :]
            b_col, b_row = b_cols[:, cf : cf + 1], b_rows[cf : cf + 1, :]
            m_prev = m_s[st][0:1, 0:1]
            log_d = jnp.where(window, b_col - b_row + i_row, -jnp.inf)
            inter = b_col + m_prev
            m_t = jnp.maximum(log_d.max(axis=-1, keepdims=True), inter)
            d_mat = jnp.exp(log_d - m_t)
            w_inter = jnp.exp(inter - m_t)
            qh = q_ref[0][:, sl]
            vh = v_ref[0][:, sl]
            s_mat = jnp.dot(qh, kt_ref[0][sl, :], preferred_element_type=F32) * d_mat
            c_t = c_s[st]
            n_row = n_s[st][0:1, :]
            num = jnp.dot(s_mat.astype(BF16), vh.astype(BF16), preferred_element_type=F32)
            num += w_inter * jnp.dot(qh, c_t.astype(BF16), preferred_element_type=F32)
            den = s_mat.sum(axis=-1, keepdims=True) + w_inter * (qh.astype(F32) * n_row).sum(axis=-1, keepdims=True)
            h_outs[d][0, :, sl] = num / jnp.maximum(jnp.abs(den), jnp.exp(-m_t))
            b_last = b_col[edge : edge + 1, :]
            ws_col = b_last - b_col + i_col
            m_new = jnp.maximum(b_last + m_prev, ws_col.max(axis=0, keepdims=True))
            decay = jnp.exp(b_last + m_prev - m_new)
            ws_col = jnp.exp(ws_col - m_new)
            c_s[st] = decay * c_t + jnp.dot(kt_ref[0][sl, :], (ws_col * vh).astype(BF16), preferred_element_type=F32)
            n_new = decay * n_row + (ws_col * k_ref[0][:, sl].astype(F32)).sum(axis=0, keepdims=True)
            n_s[st] = jnp.broadcast_to(n_new, n_s.shape[1:])
            m_s[st] = jnp.broadcast_to(m_new, m_s.shape[1:])

    @pl.when(kc == last)
    def _():
        c_out[0] = c_s[...]
        n_out[0] = n_s[...]
        m_out[0] = m_s[...]


def mlstm_bidir(qk, qk_t, z, v_col_block, gates, gates_t, bias, state):
    bsz, seq, m2 = qk.shape
    md = m2 // 2
    nh = MLSTM_HEADS
    dh = md // nh
    tt = _tile(seq, MLSTM_TILE)
    nc = seq // tt
    fwd = lambda b, k: k
    bwd = lambda b, k: nc - 1 - k
    in_specs, args = [], []
    for ck in (fwd, bwd):
        in_specs += [
            pl.BlockSpec((1, tt, md), lambda b, k, ck=ck: (b, ck(b, k), 0)),
            pl.BlockSpec((1, tt, md), lambda b, k, ck=ck: (b, ck(b, k), 1)),
            pl.BlockSpec((1, md, tt), lambda b, k, ck=ck: (b, 1, ck(b, k))),
            pl.BlockSpec((1, tt, md), lambda b, k, ck=ck: (b, ck(b, k), v_col_block)),
            pl.BlockSpec((1, tt, LANES), lambda b, k, ck=ck: (b, ck(b, k), 0)),
            pl.BlockSpec((1, 4 * nh, tt), lambda b, k, ck=ck: (b, 0, ck(b, k))),
        ]
        args += [qk, qk, qk_t, z, gates, gates_t]
    c0, n0, m0 = state
    ns = 2 * nh
    in_specs += [
        pl.BlockSpec((1, LANES), lambda b, k: (0, 0)),
        pl.BlockSpec((4 * nh, 1), lambda b, k: (0, 0)),
        pl.BlockSpec((1, ns, dh, dh), lambda b, k: (b, 0, 0, 0)),
        pl.BlockSpec((1, ns, 8, dh), lambda b, k: (b, 0, 0, 0)),
        pl.BlockSpec((1, ns, 8, LANES), lambda b, k: (b, 0, 0, 0)),
    ]
    bias_row = jnp.zeros((1, LANES), F32).at[0, : 4 * nh].set(bias)
    args += [bias_row, bias.reshape(4 * nh, 1), c0, n0, m0]
    out_specs = [
        pl.BlockSpec((1, tt, md), lambda b, k: (b, k, 0)),
        pl.BlockSpec((1, tt, md), lambda b, k: (b, nc - 1 - k, 0)),
        pl.BlockSpec((1, ns, dh, dh), lambda b, k: (b, 0, 0, 0)),
        pl.BlockSpec((1, ns, 8, dh), lambda b, k: (b, 0, 0, 0)),
        pl.BlockSpec((1, ns, 8, LANES), lambda b, k: (b, 0, 0, 0)),
    ]
    out_shape = [
        jax.ShapeDtypeStruct((bsz, seq, md), F32),
        jax.ShapeDtypeStruct((bsz, seq, md), F32),
        jax.ShapeDtypeStruct(c0.shape, F32),
        jax.ShapeDtypeStruct(n0.shape, F32),
        jax.ShapeDtypeStruct(m0.shape, F32),
    ]
    h_f, h_b, c1, n1, m1 = pl.pallas_call(
        functools.partial(_mlstm_kernel, nh, dh),
        grid=(bsz, nc),
        in_specs=in_specs,
        out_specs=out_specs,
        out_shape=out_shape,
        scratch_shapes=[pltpu.VMEM((ns, dh, dh), F32), pltpu.VMEM((ns, 8, dh), F32), pltpu.VMEM((ns, 8, LANES), F32)],
        compiler_params=_cparams("arbitrary", "arbitrary"),
    )(*args)
    return h_f, h_b, (c1, n1, m1)


def _mlstm_finish_kernel(nh, hf_ref, hb_ref, o_ref, g_ref, out_ref):
    hm = hf_ref[0] + hb_ref[0]
    dh = hm.shape[1] // nh
    gate = jax.nn.sigmoid(o_ref[0])
    g = g_ref[...]
    for h in range(nh):
        sl = slice(h * dh, (h + 1) * dh)
        xh = hm[:, sl]
        y = xh * lax.rsqrt(jnp.mean(xh * xh, axis=-1, keepdims=True) + EPS) * g[:, sl]
        out_ref[0, :, sl] = (y * gate[:, sl]).astype(out_ref.dtype)


def mlstm_finish(h_f, h_b, z, o_col_block, head_norm):
    bsz, seq, md = h_f.shape
    tl = _tile(seq, 512)
    return pl.pallas_call(
        functools.partial(_mlstm_finish_kernel, MLSTM_HEADS),
        grid=(bsz, seq // tl),
        in_specs=[
            pl.BlockSpec((1, tl, md), lambda b, i: (b, i, 0)),
            pl.BlockSpec((1, tl, md), lambda b, i: (b, i, 0)),
            pl.BlockSpec((1, tl, md), lambda b, i: (b, i, o_col_block)),
            pl.BlockSpec((1, md), lambda b, i: (0, 0)),
        ],
        out_specs=pl.BlockSpec((1, tl, md), lambda b, i: (b, i, 0)),
        out_shape=jax.ShapeDtypeStruct((bsz, seq, md), BF16),
        compiler_params=_cparams("arbitrary", "arbitrary"),
    )(h_f, h_b, z, head_norm.reshape(1, md))


def s5_matrices(lam_re, lam_im, log_step, b_re, b_im, c_re, c_im):
    tt = S5_TILE
    lam = lax.complex(lam_re.astype(F32), lam_im.astype(F32))
    step = jnp.exp(log_step.astype(F32))[..., None]
    lam_dt = lam * step
    lam_bar = jnp.exp(lam_dt)
    b_mat = lax.complex(b_re.astype(F32), b_im.astype(F32))
    b_bar = ((lam_bar - 1.0) / lam)[..., None] * b_mat[None]
    c_mat = lax.complex(c_re.astype(F32), c_im.astype(F32))
    ks = jnp.arange(tt + 1, dtype=F32)
    powers = jnp.exp(lam_dt[None] * ks[:, None, None, None])
    hp = lax.Precision.HIGHEST
    kern = jnp.real(jnp.einsum("gop,kdgp,dgpc->kdgoc", c_mat, powers[:tt], b_bar, precision=hp))
    s_idx = jnp.arange(tt)[:, None]
    t_idx = jnp.arange(tt)[None, :]
    lag_f = t_idx - s_idx
    m_f = jnp.where((lag_f >= 0)[..., None, None, None], kern[jnp.clip(lag_f, 0, tt - 1), 0], 0.0)
    m_b = jnp.where((lag_f <= 0)[..., None, None, None], kern[jnp.clip(-lag_f, 0, tt - 1), 1], 0.0)
    toep = (m_f + m_b).transpose(2, 0, 4, 1, 3)
    g_n, p_n, c_n = b_mat.shape
    toep = toep.reshape(g_n, tt * c_n, tt * c_n)
    pw_f = powers[tt - 1 - jnp.arange(tt), 0]
    pw_b = powers[jnp.arange(tt), 1]
    bin_f = pw_f[..., None] * b_bar[0][None]
    bin_b = pw_b[..., None] * b_bar[1][None]

    def lay_in(bc):
        bc = bc.transpose(1, 0, 3, 2).reshape(g_n, tt * c_n, p_n)
        re, im = jnp.real(bc), jnp.imag(bc)
        return jnp.concatenate([re, im, im, re], axis=-1)

    b_in = jnp.concatenate([lay_in(bin_f), lay_in(bin_b)], axis=-1)
    d_f = c_mat[None] * powers[1 + jnp.arange(tt), 0][:, :, None, :]
    d_b = c_mat[None] * powers[tt - jnp.arange(tt), 1][:, :, None, :]

    def lay_out(dc):
        dc = dc.transpose(1, 3, 0, 2).reshape(g_n, p_n, tt * c_n)
        return jnp.concatenate([jnp.real(dc), -jnp.imag(dc)], axis=1)

    c_out = jnp.concatenate([lay_out(d_f), lay_out(d_b)], axis=1)
    lt = powers[tt]
    lr, li = jnp.real(lt), jnp.imag(lt)
    lam_t = jnp.stack(
        [
            jnp.concatenate([lr[0], lr[0]], -1),
            jnp.concatenate([-li[0], li[0]], -1),
            jnp.concatenate([lr[1], lr[1]], -1),
            jnp.concatenate([-li[1], li[1]], -1),
        ],
        axis=1,
    )
    return toep.astype(BF16), b_in.astype(BF16), c_out.astype(BF16), lam_t


def _s5_kernel(bsz, u_ref, toep_ref, bin_ref, cout_ref, lam_ref, s0_ref, y_ref, s1_ref, v_s, st_s):
    u = u_ref[0].astype(BF16)
    v_s[...] = jnp.dot(u, bin_ref[0], preferred_element_type=F32)
    nk = u.shape[0] // bsz
    p2 = lam_ref.shape[2]
    lam = lam_ref[0]
    a_f, b_f, a_b, b_b = lam[0:1], lam[1:2], lam[2:3], lam[3:4]
    s0 = s0_ref[0]

    def body(k, carry):
        sf, sfx, sb, sbx = carry
        rf = pl.multiple_of(k * bsz, bsz)
        rb = pl.multiple_of((nk - 1 - k) * bsz, bsz)
        st_s[pl.ds(rf, bsz), 0:p2] = sf
        st_s[pl.ds(rb, bsz), p2 : 2 * p2] = sb
        vf = v_s[pl.ds(rf, bsz), 0 : 2 * p2]
        vb = v_s[pl.ds(rb, bsz), 2 * p2 : 4 * p2]
        sf_n = a_f * sf + b_f * sfx + vf[:, 0:p2]
        sfx_n = a_f * sfx - b_f * sf + vf[:, p2 : 2 * p2]
        sb_n = a_b * sb + b_b * sbx + vb[:, 0:p2]
        sbx_n = a_b * sbx - b_b * sb + vb[:, p2 : 2 * p2]
        return sf_n, sfx_n, sb_n, sbx_n

    carry = (s0[:, 0:p2], s0[:, p2 : 2 * p2], s0[:, 2 * p2 : 3 * p2], s0[:, 3 * p2 : 4 * p2])
    sf, sfx, sb, sbx = lax.fori_loop(0, nk, body, carry)
    s1_ref[0] = jnp.concatenate([sf, sfx, sb, sbx], axis=-1)
    y = jnp.dot(u, toep_ref[0], preferred_element_type=F32)
    y += jnp.dot(st_s[...].astype(BF16), cout_ref[0], preferred_element_type=F32)
    y_ref[0] = y


def s5_bidir(u_rows, mats, s0):
    toep, b_in, c_out, lam_t = mats
    g_n, rows, width = u_rows.shape
    bsz = s0.shape[1]
    p8 = b_in.shape[2]
    p2 = p8 // 4
    return pl.pallas_call(
        functools.partial(_s5_kernel, bsz),
        grid=(g_n,),
        in_specs=[
            pl.BlockSpec((1, rows, width), lambda g: (g, 0, 0)),
            pl.BlockSpec((1, width, width), lambda g: (g, 0, 0)),
            pl.BlockSpec((1, width, p8), lambda g: (g, 0, 0)),
            pl.BlockSpec((1, 2 * p2, width), lambda g: (g, 0, 0)),
            pl.BlockSpec((1, 4, p2), lambda g: (g, 0, 0)),
            pl.BlockSpec((1, bsz, p8), lambda g: (g, 0, 0)),
        ],
        out_specs=[
            pl.BlockSpec((1, rows, width), lambda g: (g, 0, 0)),
            pl.BlockSpec((1, bsz, p8), lambda g: (g, 0, 0)),
        ],
        out_shape=[jax.ShapeDtypeStruct(u_rows.shape, F32), jax.ShapeDtypeStruct(s0.shape, F32)],
        scratch_shapes=[pltpu.VMEM((rows, p8), F32), pltpu.VMEM((rows, 2 * p2), F32)],
        compiler_params=_cparams("arbitrary"),
    )(u_rows, toep, b_in, c_out, lam_t, s0)


def _s5_glu_kernel(ys_ref, u_ref, d_ref, w_ref, o_ref):
    y = ys_ref[0] + d_ref[...] * u_ref[0]
    y = jax.nn.gelu(y)
    gate = jax.nn.sigmoid(jnp.dot(y.astype(BF16), w_ref[...], preferred_element_type=F32))
    o_ref[0] = (y * gate).astype(o_ref.dtype)


def s5_glu(ys, z, u_col_block, d_skip, w_glu):
    bsz, seq, sd = ys.shape
    tl = _tile(seq, 512)
    return pl.pallas_call(
        _s5_glu_kernel,
        grid=(bsz, seq // tl),
        in_specs=[
            pl.BlockSpec((1, tl, sd), lambda b, i: (b, i, 0)),
            pl.BlockSpec((1, tl, sd), lambda b, i: (b, i, u_col_block)),
            pl.BlockSpec((1, sd), lambda b, i: (0, 0)),
            pl.BlockSpec((sd, sd), lambda b, i: (0, 0)),
        ],
        out_specs=pl.BlockSpec((1, tl, sd), lambda b, i: (b, i, 0)),
        out_shape=jax.ShapeDtypeStruct((bsz, seq, sd), BF16),
        compiler_params=_cparams("arbitrary", "arbitrary"),
    )(ys, z, d_skip.reshape(1, sd), w_glu)


def _to_s5_rows(z, u_col0, sd):
    bsz, seq = z.shape[:2]
    g_n = sd // S5_GROUP
    u = z[:, :, u_col0 : u_col0 + sd].astype(BF16).reshape(bsz, seq // S5_TILE, S5_TILE, g_n, S5_GROUP)
    return u.transpose(3, 1, 0, 2, 4).reshape(g_n, (seq // S5_TILE) * bsz, S5_TILE * S5_GROUP)


def _from_s5_rows(y, bsz):
    g_n, rows, width = y.shape
    nk = rows // bsz
    y = y.reshape(g_n, nk, bsz, S5_TILE, S5_GROUP).transpose(2, 1, 3, 0, 4)
    return y.reshape(bsz, nk * S5_TILE, g_n * S5_GROUP)


def _hybrid_prep(h, w_main, w_gates, conv_w, col_scale, md):
    bsz, seq, d = h.shape
    z = matmul([(h.reshape(bsz * seq, d), w_main)], F32).reshape(bsz, seq, -1)
    gates = matmul([(h.reshape(bsz * seq, d), w_gates)], F32, tn=LANES).reshape(bsz, seq, LANES)
    qk = qk_conv(z, conv_w, col_scale, 2 * md)
    return z, gates, qk


def hybrid_mixer(h_c, h_l, p, need_ctx):
    d = h_l.shape[2]
    md = p["head_norm"].shape[0]
    sd = d - md
    nh = MLSTM_HEADS
    dh = md // nh
    bsz = h_l.shape[0]
    g_n = sd // S5_GROUP
    v_blk, o_blk, u_blk = 2, 3, 4
    outs = []
    m_state = (
        jnp.zeros((bsz, 2 * nh, dh, dh), F32),
        jnp.zeros((bsz, 2 * nh, 8, dh), F32),
        jnp.zeros((bsz, 2 * nh, 8, LANES), F32),
    )
    s_state = jnp.zeros((g_n, bsz, 8 * p["s5_mats"][3].shape[2] // 2), F32)
    for h, want_out in ((h_c, need_ctx), (h_l, True)):
        z, gates, qk = _hybrid_prep(h, p["w_main"], p["w_gates"], p["conv"], p["col_scale"], md)
        seq = h.shape[1]
        h_f, h_b, m_state = mlstm_bidir(
            qk, jnp.swapaxes(qk, 1, 2), z, v_blk, gates, jnp.swapaxes(gates[:, :, : 4 * nh], 1, 2), p["b_gates"], m_state
        )
        ys_rows, s_state = s5_bidir(_to_s5_rows(z, u_blk * sd, sd), p["s5_mats"], s_state)
        if not want_out:
            outs.append(None)
            continue
        hm = mlstm_finish(h_f, h_b, z, o_blk, p["head_norm"])
        y = s5_glu(_from_s5_rows(ys_rows, bsz), z, u_blk, p["d_skip"], p["w_glu"])
        outs.append((hm.reshape(bsz * seq, md), y.reshape(bsz * seq, sd)))
    return outs


def kernel(x, c, ctx, c_ctx, ada_w, ada_b, norm_mix, norm_ffn, norm_final, hy_w_in, hy_b_gates, hy_conv, hy_head_norm, s5_lam_re, s5_lam_im, s5_log_step, s5_b_re, s5_b_im, s5_c_re, s5_c_im, s5_d, s5_w_glu, hy_w_out, da_w_qkv, da_lambda, da_head_norm, da_w_o, ec_router, ec_w_gate, ec_w_up, ec_w_down):
    bsz, seq, d = x.shape
    lc = ctx.shape[1]
    depth = ada_w.shape[0]
    md = hy_head_norm.shape[1]
    sd = d - md
    nh = MLSTM_HEADS
    hd = d // (2 * DIFF_HEADS)

    rows = -(-(bsz + 1) // 8) * 8
    cond = jnp.zeros((rows, d), F32).at[:bsz].set(c).at[bsz].set(c_ctx)
    mod = ada_modulation(cond, ada_w, ada_b)
    mod_l = mod[:, :bsz].reshape(depth, bsz, ADA_CHUNKS, 1, d)
    mod_c = jnp.broadcast_to(mod[:, bsz : bsz + 1].reshape(depth, 1, ADA_CHUNKS, 1, d), (depth, bsz, ADA_CHUNKS, 1, d))

    n_rows = seq // GRID_W
    row_pos = jnp.repeat(jnp.arange(n_rows, dtype=F32), GRID_W)
    col_pos = jnp.tile(jnp.arange(GRID_W, dtype=F32), n_rows)
    axis_dim = hd // 2
    inv_freq = ROPE_BASE ** (-jnp.arange(0, axis_dim, 2, dtype=F32) / axis_dim)
    ar = row_pos[:, None] * inv_freq[None]
    ac = col_pos[:, None] * inv_freq[None]
    ang = jnp.concatenate([ar, ar, ac, ac], axis=-1)
    quarter = hd // 4
    sign = jnp.where((jnp.arange(hd) % (2 * quarter)) < quarter, -1.0, 1.0)
    cos_t, sin_t = jnp.cos(ang), jnp.sin(ang) * sign

    for layer in range(depth):
        need_ctx = layer < depth - 1
        j = layer // 2
        ml = [mod_l[layer, :, i] for i in range(ADA_CHUNKS)]
        mc = [mod_c[layer, :, i] for i in range(ADA_CHUNKS)]
        (h_l,) = norm_modulate(x, norm_mix[layer], ml[1], ml[0], [BF16])
        (h_c,) = norm_modulate(ctx, norm_mix[layer], mc[1], mc[0], [BF16])
        if layer % 2 == 0:
            w_in = hy_w_in[j]
            g0 = 4 * md
            g1 = g0 + 4 * nh
            w_main = jnp.concatenate([w_in[:, :g0], w_in[:, g1:]], axis=1).astype(BF16)
            w_gates = jnp.zeros((d, LANES), F32).at[:, : 4 * nh].set(w_in[:, g0:g1]).astype(BF16)
            col_scale = jnp.concatenate([jnp.ones((md,), F32), jnp.full((md,), (md // nh) ** -0.5, F32)]).reshape(1, 2 * md)
            p = dict(
                w_main=w_main, w_gates=w_gates, conv=hy_conv[j], col_scale=col_scale, b_gates=hy_b_gates[j],
                head_norm=hy_head_norm[j], d_skip=s5_d[j], w_glu=s5_w_glu[j].astype(BF16),
                s5_mats=s5_matrices(s5_lam_re[j], s5_lam_im[j], s5_log_step[j], s5_b_re[j], s5_b_im[j], s5_c_re[j], s5_c_im[j]),
            )
            mix_c, mix_l = hybrid_mixer(h_c, h_l, p, need_ctx)
            w_out = hy_w_out[j].astype(BF16)
            pairs_l = [(mix_l[0], w_out[:md]), (mix_l[1], w_out[md:])]
            pairs_c = [(mix_c[0], w_out[:md]), (mix_c[1], w_out[md:])] if need_ctx else None
        else:
            lam_init = 0.8 - 0.6 * math.exp(-0.3 * layer)
            lv = da_lambda[j].astype(F32)
            lam = jnp.exp(jnp.sum(lv[0] * lv[1])) - jnp.exp(jnp.sum(lv[2] * lv[3])) + lam_init
            w_qkv = da_w_qkv[j].astype(BF16)
            q_scale = jnp.concatenate([jnp.full((d,), hd**-0.5 * math.log2(math.e), F32), jnp.ones((2 * d,), F32)]).reshape(1, 3 * d)
            qkv_l = matmul([(h_l.reshape(bsz * seq, d), w_qkv)], BF16, rope=(cos_t, sin_t, 2 * d, seq), col_scale=q_scale).reshape(bsz, seq, 3 * d)
            qkv_c = matmul([(h_c.reshape(bsz * lc, d), w_qkv)], BF16, col_scale=q_scale).reshape(bsz, lc, 3 * d)
            o_l = diff_attention(qkv_l, [qkv_c, qkv_l], lam, da_head_norm[j], 1.0 - lam_init)
            w_o = da_w_o[j].astype(BF16)
            pairs_l = [(o_l.reshape(bsz * seq, d), w_o)]
            pairs_c = None
            if need_ctx:
                o_c = diff_attention(qkv_c, [qkv_c], lam, da_head_norm[j], 1.0 - lam_init)
                pairs_c = [(o_c.reshape(bsz * lc, d), w_o)]
        x = matmul(pairs_l, F32, residual=x.reshape(bsz * seq, d), gate=ml[2], rows_per_batch=seq).reshape(bsz, seq, d)
        w_rt = ec_router[layer].T.astype(BF16)
        w_g, w_u, w_d = (w[layer].astype(BF16) for w in (ec_w_gate, ec_w_up, ec_w_down))
        x = expert_choice_ffn(x, norm_ffn[layer], ml[4], ml[3], ml[5], w_rt, w_g, w_u, w_d)
        if need_ctx:
            ctx = matmul(pairs_c, F32, residual=ctx.reshape(bsz * lc, d), gate=mc[2], rows_per_batch=lc).reshape(bsz, lc, d)
            ctx = expert_choice_ffn(ctx, norm_ffn[layer], mc[4], mc[3], mc[5], w_rt, w_g, w_u, w_d)
    return final_norm(x, norm_final)
```

```python
import functools
import math

import jax
import jax.numpy as jnp
from jax import lax
from jax.experimental import pallas as pl
from jax.experimental.pallas import tpu as pltpu

F32 = jnp.float32
BF16 = jnp.bfloat16

GRID_W = 64
EPS = 1e-6
ADA_CHUNKS = 6
MLSTM_HEADS = 4
QK_CONV = 3
S5_GROUP = 16
DIFF_HEADS = 8
ROPE_BASE = 10000.0
CAPACITY_FACTOR = 2

MLSTM_TILE = 256
S5_TILE = 16
VMEM_LIMIT_BYTES = 56 * 1024 * 1024
ROW_TILE = 1024
COL_TILE = 1024
RESIDUAL_ROW_TILE = 512
ATTN_Q_TILE = 1024
ATTN_SUB_TILE = 256
ATTN_SCORE_SLOTS = 3
ATTN_KEY_CHUNK = 512
LANES = 128
SUBLANES = 8


def _cparams(*sem):
    return pltpu.CompilerParams(dimension_semantics=sem, vmem_limit_bytes=VMEM_LIMIT_BYTES)


def _tile(n, pref):
    t = min(n, pref)
    while n % t:
        t //= 2
    return t


def _ada_kernel(c_ref, w_ref, b_ref, o_ref):
    cv = c_ref[...]
    a = (cv * jax.nn.sigmoid(cv)).astype(BF16)
    o_ref[0] = jnp.dot(a, w_ref[0].astype(BF16), preferred_element_type=F32) + b_ref[0]


def ada_modulation(cond, ada_w, ada_b):
    depth, d, n = ada_w.shape
    rows = cond.shape[0]
    tn = _tile(n, COL_TILE)
    return pl.pallas_call(
        _ada_kernel,
        grid=(depth, n // tn),
        in_specs=[
            pl.BlockSpec((rows, d), lambda l, j: (0, 0)),
            pl.BlockSpec((1, d, tn), lambda l, j: (l, 0, j)),
            pl.BlockSpec((1, 1, tn), lambda l, j: (l, 0, j)),
        ],
        out_specs=pl.BlockSpec((1, rows, tn), lambda l, j: (l, 0, j)),
        out_shape=jax.ShapeDtypeStruct((depth, rows, n), F32),
        compiler_params=_cparams("arbitrary", "arbitrary"),
    )(cond, ada_w, ada_b.reshape(depth, 1, n))


def _normmod_kernel(x_ref, g_ref, sc_ref, sh_ref, *o_refs):
    x = x_ref[0]
    y = x * lax.rsqrt(jnp.mean(x * x, axis=-1, keepdims=True) + EPS) * g_ref[...]
    y = y * (1.0 + sc_ref[0]) + sh_ref[0]
    for o_ref in o_refs:
        o_ref[0] = y.astype(o_ref.dtype)


def norm_modulate(x, g, scale, shift, dtypes):
    bsz, seq, d = x.shape
    tl = _tile(seq, 512)
    return pl.pallas_call(
        _normmod_kernel,
        grid=(bsz, seq // tl),
        in_specs=[
            pl.BlockSpec((1, tl, d), lambda b, i: (b, i, 0)),
            pl.BlockSpec((1, d), lambda b, i: (0, 0)),
            pl.BlockSpec((1, 1, d), lambda b, i: (b, 0, 0)),
            pl.BlockSpec((1, 1, d), lambda b, i: (b, 0, 0)),
        ],
        out_specs=[pl.BlockSpec((1, tl, d), lambda b, i: (b, i, 0)) for _ in dtypes],
        out_shape=[jax.ShapeDtypeStruct(x.shape, dt) for dt in dtypes],
        compiler_params=_cparams("arbitrary", "arbitrary"),
    )(x, g.reshape(1, d), scale, shift)


def _final_norm_kernel(x_ref, g_ref, o_ref):
    x = x_ref[0]
    o_ref[0] = x * lax.rsqrt(jnp.mean(x * x, axis=-1, keepdims=True) + EPS) * g_ref[...]


def final_norm(x, g):
    bsz, seq, d = x.shape
    tl = _tile(seq, 512)
    return pl.pallas_call(
        _final_norm_kernel,
        grid=(bsz, seq // tl),
        in_specs=[pl.BlockSpec((1, tl, d), lambda b, i: (b, i, 0)), pl.BlockSpec((1, d), lambda b, i: (0, 0))],
        out_specs=pl.BlockSpec((1, tl, d), lambda b, i: (b, i, 0)),
        out_shape=jax.ShapeDtypeStruct(x.shape, F32),
        compiler_params=_cparams("arbitrary", "arbitrary"),
    )(x, g.reshape(1, d))


def _mm_kernel(n_pairs, *refs):
    o_ref = refs[2 * n_pairs]
    acc = jnp.dot(refs[0][...], refs[1][...], preferred_element_type=F32)
    for p in range(1, n_pairs):
        acc += jnp.dot(refs[2 * p][...], refs[2 * p + 1][...], preferred_element_type=F32)
    o_ref[...] = acc.astype(o_ref.dtype)


def _mm_colscale_kernel(a_ref, w_ref, cs_ref, o_ref):
    acc = jnp.dot(a_ref[...], w_ref[...], preferred_element_type=F32)
    o_ref[...] = (acc * cs_ref[...]).astype(o_ref.dtype)


def _mm_residual_kernel(n_pairs, *refs):
    r_ref, g_ref, o_ref = refs[2 * n_pairs : 2 * n_pairs + 3]
    acc = jnp.dot(refs[0][...], refs[1][...], preferred_element_type=F32)
    for p in range(1, n_pairs):
        acc += jnp.dot(refs[2 * p][...], refs[2 * p + 1][...], preferred_element_type=F32)
    o_ref[...] = r_ref[...] + g_ref[0] * acc


def _mm_rope_kernel(n_rope_tiles, quarter, a_ref, w_ref, cs_ref, cos_ref, sin_ref, o_ref):
    acc = jnp.dot(a_ref[...], w_ref[...], preferred_element_type=F32) * cs_ref[...]
    j = pl.program_id(1)

    @pl.when(j >= n_rope_tiles)
    def _():
        o_ref[...] = acc.astype(o_ref.dtype)

    @pl.when(j < n_rope_tiles)
    def _():
        cos = cos_ref[...]
        sin = sin_ref[...]
        hd = cos.shape[1]
        lane = lax.broadcasted_iota(jnp.int32, (acc.shape[0], hd), 1)
        first = (lane % (2 * quarter)) < quarter
        for g in range(acc.shape[1] // hd):
            xg = acc[:, g * hd : (g + 1) * hd]
            up = pltpu.roll(xg, hd - quarter, 1)
            dn = pltpu.roll(xg, quarter, 1)
            o_ref[:, g * hd : (g + 1) * hd] = (xg * cos + jnp.where(first, up, dn) * sin).astype(o_ref.dtype)


def matmul(pairs, out_dtype, *, tm=ROW_TILE, tn=COL_TILE, residual=None, gate=None, rows_per_batch=None, rope=None,
           col_scale=None):
    m = pairs[0][0].shape[0]
    n = pairs[0][1].shape[1]
    if rope is not None:
        rows_per_batch = rope[3]
    tm = _tile(m if rows_per_batch is None else rows_per_batch, tm)
    tn = _tile(n, tn)
    grid = (m // tm, n // tn)
    in_specs, args = [], []
    for a, w in pairs:
        k = a.shape[1]
        in_specs += [pl.BlockSpec((tm, k), lambda i, j: (i, 0)), pl.BlockSpec((k, tn), lambda i, j: (0, j))]
        args += [a, w]
    out_spec = pl.BlockSpec((tm, tn), lambda i, j: (i, j))
    aliases = {}
    if residual is not None:
        bpt = rows_per_batch // tm
        in_specs += [pl.BlockSpec((tm, tn), lambda i, j: (i, j)), pl.BlockSpec((1, 1, tn), lambda i, j: (i // bpt, 0, j))]
        args += [residual, gate]
        aliases = {2 * len(pairs): 0}
        body = functools.partial(_mm_residual_kernel, len(pairs))
    elif rope is not None:
        cos, sin, n_rope_cols, rpb = rope
        hd = cos.shape[1]
        bpt = rpb // tm
        in_specs += [
            pl.BlockSpec((1, tn), lambda i, j: (0, j)),
            pl.BlockSpec((tm, hd), lambda i, j: (i % bpt, 0)),
            pl.BlockSpec((tm, hd), lambda i, j: (i % bpt, 0)),
        ]
        args += [col_scale, cos, sin]
        body = functools.partial(_mm_rope_kernel, n_rope_cols // tn, hd // 4)
    elif col_scale is not None:
        in_specs.append(pl.BlockSpec((1, tn), lambda i, j: (0, j)))
        args.append(col_scale)
        body = _mm_colscale_kernel
    else:
        body = functools.partial(_mm_kernel, len(pairs))
    return pl.pallas_call(
        body,
        grid=grid,
        in_specs=in_specs,
        out_specs=out_spec,
        out_shape=jax.ShapeDtypeStruct((m, n), out_dtype),
        input_output_aliases=aliases,
        compiler_params=_cparams("arbitrary", "arbitrary"),
    )(*args)


def _attn_kernel(n_seg, seg_off, hd, sub, out_scale, lam_ref, q_ref, *refs):
    kv_refs = refs[: 2 * n_seg]
    g_ref, o_ref, s_scr, p_scr = refs[2 * n_seg : 2 * n_seg + 4]
    lam = lam_ref[0]
    nt = (((1,), (1,)), ((), ()))
    n_sub = q_ref.shape[1] // sub
    n_chain = 2 * n_sub
    n_s, n_p = s_scr.shape[0], p_scr.shape[0]
    row_max, row_sum, outs = {}, {}, {}
    max_w = math.gcd(LANES, *[seg_off[s + 1] - seg_off[s] for s in range(n_seg)])

    def stage_scores(c):
        r, i = divmod(c, 2)
        qi = q_ref[0, r * sub : (r + 1) * sub, i * hd : (i + 1) * hd]
        m_lanes = None
        for s in range(n_seg):
            seg_len = seg_off[s + 1] - seg_off[s]
            step = min(ATTN_KEY_CHUNK, seg_len)
            for k0 in range(0, seg_len, step):
                kc = kv_refs[2 * s][0, k0 : k0 + step, i * hd : (i + 1) * hd]
                sc = lax.dot_general(qi, kc, nt, preferred_element_type=F32)
                s_scr[c % n_s, :, seg_off[s] + k0 : seg_off[s] + k0 + step] = sc
                for l0 in range(0, step, max_w):
                    piece = sc[:, l0 : l0 + max_w]
                    m_lanes = piece if m_lanes is None else jnp.maximum(m_lanes, piece)
        row_max[c] = m_lanes.max(axis=-1, keepdims=True)

    def stage_probs(c):
        p = jnp.exp2(s_scr[c % n_s] - row_max[c])
        row_sum[c] = p.sum(axis=-1, keepdims=True)
        p_scr[c % n_p] = p.astype(BF16)

    def stage_values(c):
        acc = None
        for s in range(n_seg):
            lo, hi = seg_off[s], seg_off[s + 1]
            part = jnp.dot(p_scr[c % n_p, :, lo:hi], kv_refs[2 * s + 1][0], preferred_element_type=F32)
            acc = part if acc is None else acc + part
        outs[c] = acc / row_sum[c]
        if c % 2 == 1:
            r = c // 2
            o = outs.pop(c - 1) - lam * outs.pop(c)
            o = o * lax.rsqrt(jnp.mean(o * o, axis=-1, keepdims=True) + EPS) * g_ref[...] * out_scale
            o_ref[0, r * sub : (r + 1) * sub, :] = o.astype(o_ref.dtype)

    for t in range(n_chain + 2):
        if t < n_chain:
            stage_scores(t)
        if 0 <= t - 1 < n_chain:
            stage_probs(t - 1)
        if 0 <= t - 2 < n_chain:
            stage_values(t - 2)


def diff_attention(q_src, kv_srcs, lam, head_norm, out_scale):
    bsz, lq, d3 = q_src.shape
    d = d3 // 3
    hw = d // DIFF_HEADS
    hd = hw // 2
    tq = _tile(lq, ATTN_Q_TILE)
    sub = _tile(tq, ATTN_SUB_TILE)
    seg_off = [0]
    for src in kv_srcs:
        seg_off.append(seg_off[-1] + src.shape[1])
    in_specs = [
        pl.BlockSpec(memory_space=pltpu.SMEM),
        pl.BlockSpec((1, tq, hw), lambda b, h, i: (b, i, h)),
    ]
    args = [lam.reshape(1), q_src]
    for src in kv_srcs:
        ls = src.shape[1]
        in_specs += [
            pl.BlockSpec((1, ls, hw), lambda b, h, i: (b, 0, DIFF_HEADS + h)),
            pl.BlockSpec((1, ls, hw), lambda b, h, i: (b, 0, 2 * DIFF_HEADS + h)),
        ]
        args += [src, src]
    in_specs.append(pl.BlockSpec((1, hw), lambda b, h, i: (0, h)))
    args.append(head_norm.reshape(1, d))
    return pl.pallas_call(
        functools.partial(_attn_kernel, len(kv_srcs), tuple(seg_off), hd, sub, out_scale),
        grid=(bsz, DIFF_HEADS, lq // tq),
        in_specs=in_specs,
        out_specs=pl.BlockSpec((1, tq, hw), lambda b, h, i: (b, i, h)),
        out_shape=jax.ShapeDtypeStruct((bsz, lq, d), BF16),
        scratch_shapes=[
            pltpu.VMEM((ATTN_SCORE_SLOTS, sub, seg_off[-1]), F32),
            pltpu.VMEM((ATTN_SCORE_SLOTS, sub, seg_off[-1]), BF16),
        ],
        compiler_params=_cparams("arbitrary", "arbitrary", "arbitrary"),
    )(*args)


def _rms_modulate(x, g, scale, shift):
    y = x * lax.rsqrt(jnp.mean(x * x, axis=-1, keepdims=True) + EPS) * g
    return y * (1.0 + scale) + shift


def _normmod_router_kernel(x_ref, g_ref, sc_ref, sh_ref, w_ref, h_ref, aff_ref):
    y = _rms_modulate(x_ref[0], g_ref[...], sc_ref[0], sh_ref[0])
    h_ref[0] = y
    logits = jnp.dot(y.astype(BF16), w_ref[...], preferred_element_type=F32).T[: aff_ref.shape[1]]
    z = jnp.exp(logits - logits.max(axis=0, keepdims=True))
    aff_ref[0] = z / z.sum(axis=0, keepdims=True)


def norm_modulate_router(x, g, scale, shift, w_router, ne):
    bsz, seq, d = x.shape
    tl = _tile(seq, 512)
    return pl.pallas_call(
        _normmod_router_kernel,
        grid=(bsz, seq // tl),
        in_specs=[
            pl.BlockSpec((1, tl, d), lambda b, i: (b, i, 0)),
            pl.BlockSpec((1, d), lambda b, i: (0, 0)),
            pl.BlockSpec((1, 1, d), lambda b, i: (b, 0, 0)),
            pl.BlockSpec((1, 1, d), lambda b, i: (b, 0, 0)),
            pl.BlockSpec((d, LANES), lambda b, i: (0, 0)),
        ],
        out_specs=[pl.BlockSpec((1, tl, d), lambda b, i: (b, i, 0)), pl.BlockSpec((1, ne, tl), lambda b, i: (b, 0, i))],
        out_shape=[jax.ShapeDtypeStruct(x.shape, F32), jax.ShapeDtypeStruct((bsz, ne, seq), F32)],
        compiler_params=_cparams("arbitrary", "arbitrary"),
    )(x, g.reshape(1, d), scale, shift, w_router)


def _moe_kernel(cap, nb, idx_ref, idx_next_ref, gate_ref, mod_ref, h_hbm, x_in, wg_ref, wu_ref, wd_ref, x_out,
                xs_buf, xr_buf, sems):
    del x_in
    e, b = pl.program_id(0), pl.program_id(1)
    n = e * nb + b
    n_last = pl.num_programs(0) * nb - 1
    slot = n % 2
    other = 1 - slot
    sem_h, sem_in, sem_out = 0, 1, 2

    def start_rows(rows_of, idx, buf_slot, sem, to_hbm=False):
        def body(g, carry):
            for k in range(SUBLANES):
                hbm_row = rows_of.at[pl.ds(idx[0, 0, g * SUBLANES + k], 1)]
                buf_row = buf_slot.at[g, pl.ds(k, 1)]
                if to_hbm:
                    pltpu.make_async_copy(buf_row, hbm_row, sem).start()
                else:
                    pltpu.make_async_copy(hbm_row, buf_row, sem).start()
            return carry

        lax.fori_loop(0, cap // SUBLANES, body, 0)

    def wait_all(buf_slot, sem):
        pltpu.make_async_copy(buf_slot, buf_slot, sem).wait()

    @pl.when(n == 0)
    def _():
        start_rows(h_hbm.at[b], idx_ref, xs_buf.at[slot], sems.at[sem_h, slot])

    @pl.when(n < n_last)
    def _():
        start_rows(h_hbm.at[(b + 1) % nb], idx_next_ref, xs_buf.at[other], sems.at[sem_h, other])

    start_rows(x_out.at[b], idx_ref, xr_buf.at[slot], sems.at[sem_in, slot])

    wait_all(xs_buf.at[slot], sems.at[sem_h, slot])
    d = xs_buf.shape[-1]
    xs = xs_buf[slot].reshape(cap, d).astype(BF16)
    gt = jnp.dot(xs, wg_ref[0], preferred_element_type=F32)
    up = jnp.dot(xs, wu_ref[0], preferred_element_type=F32)
    hid = (gt * jax.nn.sigmoid(gt) * up).astype(BF16)
    out = jnp.dot(hid, wd_ref[0], preferred_element_type=F32) * gate_ref[0] * mod_ref[0]
    wait_all(xr_buf.at[slot], sems.at[sem_in, slot])
    xr_buf[slot] += out.reshape(cap // SUBLANES, SUBLANES, d)

    @pl.when(n > 0)
    def _():
        wait_all(xr_buf.at[other], sems.at[sem_out, other])

    start_rows(x_out.at[b], idx_ref, xr_buf.at[slot], sems.at[sem_out, slot], to_hbm=True)

    @pl.when(n == n_last)
    def _():
        wait_all(xr_buf.at[slot], sems.at[sem_out, slot])


def moe_apply(x, h, idx, gate, mod, w_gate, w_up, w_down):
    bsz, seq, d = x.shape
    ne, cap = idx.shape[1], idx.shape[2]
    ff = w_gate.shape[2]
    idx_rows = idx.reshape(bsz * ne, 1, cap)

    def next_rows(e, b):
        nxt = jnp.minimum(e * bsz + b + 1, ne * bsz - 1)
        return ((nxt % bsz) * ne + nxt // bsz, 0, 0)

    single = pl.Buffered(1)
    return pl.pallas_call(
        functools.partial(_moe_kernel, cap, bsz),
        grid=(ne, bsz),
        in_specs=[
            pl.BlockSpec((1, 1, cap), lambda e, b: (b * ne + e, 0, 0), memory_space=pltpu.SMEM),
            pl.BlockSpec((1, 1, cap), next_rows, memory_space=pltpu.SMEM),
            pl.BlockSpec((1, cap, 1), lambda e, b: (b * ne + e, 0, 0)),
            pl.BlockSpec((1, 1, d), lambda e, b: (b, 0, 0)),
            pl.BlockSpec(memory_space=pl.ANY),
            pl.BlockSpec(memory_space=pl.ANY),
            pl.BlockSpec((1, d, ff), lambda e, b: (e, 0, 0), pipeline_mode=single),
            pl.BlockSpec((1, d, ff), lambda e, b: (e, 0, 0), pipeline_mode=single),
            pl.BlockSpec((1, ff, d), lambda e, b: (e, 0, 0), pipeline_mode=single),
        ],
        out_specs=pl.BlockSpec(memory_space=pl.ANY),
        out_shape=jax.ShapeDtypeStruct(x.shape, F32),
        scratch_shapes=[
            pltpu.VMEM((2, cap // SUBLANES, SUBLANES, d), F32),
            pltpu.VMEM((2, cap // SUBLANES, SUBLANES, d), F32),
            pltpu.SemaphoreType.DMA((3, 2)),
        ],
        input_output_aliases={5: 0},
        compiler_params=_cparams("arbitrary", "arbitrary"),
    )(idx_rows, idx_rows, gate.reshape(bsz * ne, cap, 1), mod, h, x, w_gate, w_up, w_down)


def expert_choice_ffn(x, norm_g, scale, shift, mod, w_router, w_gate, w_up, w_down):
    seq = x.shape[1]
    ne = w_gate.shape[0]
    cap = CAPACITY_FACTOR * seq // ne
    h, aff = norm_modulate_router(x, norm_g, scale, shift, w_router, ne)
    gate, idx = lax.top_k(aff, cap)
    return moe_apply(x, h, idx.astype(jnp.int32), gate, mod, w_gate, w_up, w_down)


def _conv_kernel(z_ref, w_ref, s_ref, o_ref):
    x = z_ref[0]
    seq = x.shape[0]
    row = lax.broadcasted_iota(jnp.int32, x.shape, 0)
    prev = jnp.where(row == 0, 0.0, pltpu.roll(x, 1, 0))
    nxt = jnp.where(row == seq - 1, 0.0, pltpu.roll(x, seq - 1, 0))
    w = w_ref[...]
    y = prev * w[0:1] + x * w[1:2] + nxt * w[2:3]
    o_ref[0] = (y * jax.nn.sigmoid(y) * s_ref[...]).astype(o_ref.dtype)


def qk_conv(z, conv_w, col_scale, n_cols):
    bsz, seq = z.shape[:2]
    tc = 256
    return pl.pallas_call(
        _conv_kernel,
        grid=(bsz, n_cols // tc),
        in_specs=[
            pl.BlockSpec((1, seq, tc), lambda b, j: (b, 0, j)),
            pl.BlockSpec((QK_CONV, tc), lambda b, j: (0, j)),
            pl.BlockSpec((1, tc), lambda b, j: (0, j)),
        ],
        out_specs=pl.BlockSpec((1, seq, tc), lambda b, j: (b, 0, j)),
        out_shape=jax.ShapeDtypeStruct((bsz, seq, n_cols), BF16),
        compiler_params=_cparams("arbitrary", "arbitrary"),
    )(z, conv_w, col_scale)


def _log_sigmoid(x):
    return jnp.minimum(x, 0.0) - jnp.log(1.0 + jnp.exp(-jnp.abs(x)))


def _mlstm_kernel(nh, dh, *refs):
    dir_refs = (refs[0:6], refs[6:12])
    bias_ref, bias_t_ref, c0_ref, n0_ref, m0_ref = refs[12:17]
    h_outs = refs[17:19]
    c_out, n_out, m_out = refs[19:22]
    c_s, n_s, m_s = refs[22:25]
    kc = pl.program_id(1)
    last = pl.num_programs(1) - 1

    @pl.when(kc == 0)
    def _():
        c_s[...] = c0_ref[0]
        n_s[...] = n0_ref[0]
        m_s[...] = m0_ref[0]

    tt = dir_refs[0][0].shape[1]
    row = lax.broadcasted_iota(jnp.int32, (tt, tt), 0)
    col = lax.broadcasted_iota(jnp.int32, (tt, tt), 1)
    nt = (((1,), (1,)), ((), ()))
    hi = lax.Precision.HIGHEST
    for d in range(2):
        q_ref, k_ref, kt_ref, v_ref, g_ref, gt_ref = dir_refs[d]
        window = (col <= row) if d == 0 else (col >= row)
        tri = window.astype(F32)
        gates = g_ref[0] + bias_ref[...]
        gates_t = gt_ref[0] + bias_t_ref[...]
        b_cols = jnp.dot(tri, _log_sigmoid(gates), precision=hi, preferred_element_type=F32)
        b_rows = lax.dot_general(_log_sigmoid(gates_t), tri, nt, precision=hi, preferred_element_type=F32)
        edge = tt - 1 if d == 0 else 0
        for h in range(nh):
            ci = 2 * nh * d + h
            cf = ci + nh
            sl = slice(h * dh, (h + 1) * dh)
            st = d * nh + h
            i_col, i_row = gates[:, ci : ci + 1], gates_t[ci : ci + 1, :]
            b_col, b_row = b_cols[:, cf : cf + 1], b_rows[cf : cf + 1, :]
            m_prev = m_s[st][0:1, 0:1]
            log_d = jnp.where(window, b_col - b_row + i_row, -jnp.inf)
            inter = b_col + m_prev
            m_t = jnp.maximum(log_d.max(axis=-1, keepdims=True), inter)
            d_mat = jnp.exp(log_d - m_t)
            w_inter = jnp.exp(inter - m_t)
            qh = q_ref[0][:, sl]
            vh = v_ref[0][:, sl]
            s_mat = jnp.dot(qh, kt_ref[0][sl, :], preferred_element_type=F32) * d_mat
            c_t = c_s[st]
            n_row = n_s[st][0:1, :]
            num = jnp.dot(s_mat.astype(BF16), vh.astype(BF16), preferred_element_type=F32)
            num += w_inter * jnp.dot(qh, c_t.astype(BF16), preferred_element_type=F32)
            den = s_mat.sum(axis=-1, keepdims=True) + w_inter * (qh.astype(F32) * n_row).sum(axis=-1, keepdims=True)
            h_outs[d][0, :, sl] = num / jnp.maximum(jnp.abs(den), jnp.exp(-m_t))
            b_last = b_col[edge : edge + 1, :]
            ws_col = b_last - b_col + i_col
            m_new = jnp.maximum(b_last + m_prev, ws_col.max(axis=0, keepdims=True))
            decay = jnp.exp(b_last + m_prev - m_new)
            ws_col = jnp.exp(ws_col - m_new)
            c_s[st] = decay * c_t + jnp.dot(kt_ref[0][sl, :], (ws_col * vh).astype(BF16), preferred_element_type=F32)
            n_new = decay * n_row + (ws_col * k_ref[0][:, sl].astype(F32)).sum(axis=0, keepdims=True)
            n_s[st] = jnp.broadcast_to(n_new, n_s.shape[1:])
            m_s[st] = jnp.broadcast_to(m_new, m_s.shape[1:])

    @pl.when(kc == last)
    def _():
        c_out[0] = c_s[...]
        n_out[0] = n_s[...]
        m_out[0] = m_s[...]


def mlstm_bidir(qk, qk_t, z, v_col_block, gates, gates_t, bias, state):
    bsz, seq, m2 = qk.shape
    md = m2 // 2
    nh = MLSTM_HEADS
    dh = md // nh
    tt = _tile(seq, MLSTM_TILE)
    nc = seq // tt
    fwd = lambda b, k: k
    bwd = lambda b, k: nc - 1 - k
    in_specs, args = [], []
    for ck in (fwd, bwd):
        in_specs += [
            pl.BlockSpec((1, tt, md), lambda b, k, ck=ck: (b, ck(b, k), 0)),
            pl.BlockSpec((1, tt, md), lambda b, k, ck=ck: (b, ck(b, k), 1)),
            pl.BlockSpec((1, md, tt), lambda b, k, ck=ck: (b, 1, ck(b, k))),
            pl.BlockSpec((1, tt, md), lambda b, k, ck=ck: (b, ck(b, k), v_col_block)),
            pl.BlockSpec((1, tt, LANES), lambda b, k, ck=ck: (b, ck(b, k), 0)),
            pl.BlockSpec((1, 4 * nh, tt), lambda b, k, ck=ck: (b, 0, ck(b, k))),
        ]
        args += [qk, qk, qk_t, z, gates, gates_t]
    c0, n0, m0 = state
    ns = 2 * nh
    in_specs += [
        pl.BlockSpec((1, LANES), lambda b, k: (0, 0)),
        pl.BlockSpec((4 * nh, 1), lambda b, k: (0, 0)),
        pl.BlockSpec((1, ns, dh, dh), lambda b, k: (b, 0, 0, 0)),
        pl.BlockSpec((1, ns, 8, dh), lambda b, k: (b, 0, 0, 0)),
        pl.BlockSpec((1, ns, 8, LANES), lambda b, k: (b, 0, 0, 0)),
    ]
    bias_row = jnp.zeros((1, LANES), F32).at[0, : 4 * nh].set(bias)
    args += [bias_row, bias.reshape(4 * nh, 1), c0, n0, m0]
    out_specs = [
        pl.BlockSpec((1, tt, md), lambda b, k: (b, k, 0)),
        pl.BlockSpec((1, tt, md), lambda b, k: (b, nc - 1 - k, 0)),
        pl.BlockSpec((1, ns, dh, dh), lambda b, k: (b, 0, 0, 0)),
        pl.BlockSpec((1, ns, 8, dh), lambda b, k: (b, 0, 0, 0)),
        pl.BlockSpec((1, ns, 8, LANES), lambda b, k: (b, 0, 0, 0)),
    ]
    out_shape = [
        jax.ShapeDtypeStruct((bsz, seq, md), F32),
        jax.ShapeDtypeStruct((bsz, seq, md), F32),
        jax.ShapeDtypeStruct(c0.shape, F32),
        jax.ShapeDtypeStruct(n0.shape, F32),
        jax.ShapeDtypeStruct(m0.shape, F32),
    ]
    h_f, h_b, c1, n1, m1 = pl.pallas_call(
        functools.partial(_mlstm_kernel, nh, dh),
        grid=(bsz, nc),
        in_specs=in_specs,
        out_specs=out_specs,
        out_shape=out_shape,
        scratch_shapes=[pltpu.VMEM((ns, dh, dh), F32), pltpu.VMEM((ns, 8, dh), F32), pltpu.VMEM((ns, 8, LANES), F32)],
        compiler_params=_cparams("arbitrary", "arbitrary"),
    )(*args)
    return h_f, h_b, (c1, n1, m1)


def _mlstm_finish_kernel(nh, hf_ref, hb_ref, o_ref, g_ref, out_ref):
    hm = hf_ref[0] + hb_ref[0]
    dh = hm.shape[1] // nh
    gate = jax.nn.sigmoid(o_ref[0])
    g = g_ref[...]
    for h in range(nh):
        sl = slice(h * dh, (h + 1) * dh)
        xh = hm[:, sl]
        y = xh * lax.rsqrt(jnp.mean(xh * xh, axis=-1, keepdims=True) + EPS) * g[:, sl]
        out_ref[0, :, sl] = (y * gate[:, sl]).astype(out_ref.dtype)


def mlstm_finish(h_f, h_b, z, o_col_block, head_norm):
    bsz, seq, md = h_f.shape
    tl = _tile(seq, 512)
    return pl.pallas_call(
        functools.partial(_mlstm_finish_kernel, MLSTM_HEADS),
        grid=(bsz, seq // tl),
        in_specs=[
            pl.BlockSpec((1, tl, md), lambda b, i: (b, i, 0)),
            pl.BlockSpec((1, tl, md), lambda b, i: (b, i, 0)),
            pl.BlockSpec((1, tl, md), lambda b, i: (b, i, o_col_block)),
            pl.BlockSpec((1, md), lambda b, i: (0, 0)),
        ],
        out_specs=pl.BlockSpec((1, tl, md), lambda b, i: (b, i, 0)),
        out_shape=jax.ShapeDtypeStruct((bsz, seq, md), BF16),
        compiler_params=_cparams("arbitrary", "arbitrary"),
    )(h_f, h_b, z, head_norm.reshape(1, md))


def s5_matrices(lam_re, lam_im, log_step, b_re, b_im, c_re, c_im):
    tt = S5_TILE
    lam = lax.complex(lam_re.astype(F32), lam_im.astype(F32))
    step = jnp.exp(log_step.astype(F32))[..., None]
    lam_dt = lam * step
    lam_bar = jnp.exp(lam_dt)
    b_mat = lax.complex(b_re.astype(F32), b_im.astype(F32))
    b_bar = ((lam_bar - 1.0) / lam)[..., None] * b_mat[None]
    c_mat = lax.complex(c_re.astype(F32), c_im.astype(F32))
    ks = jnp.arange(tt + 1, dtype=F32)
    powers = jnp.exp(lam_dt[None] * ks[:, None, None, None])
    hp = lax.Precision.HIGHEST
    kern = jnp.real(jnp.einsum("gop,kdgp,dgpc->kdgoc", c_mat, powers[:tt], b_bar, precision=hp))
    s_idx = jnp.arange(tt)[:, None]
    t_idx = jnp.arange(tt)[None, :]
    lag_f = t_idx - s_idx
    m_f = jnp.where((lag_f >= 0)[..., None, None, None], kern[jnp.clip(lag_f, 0, tt - 1), 0], 0.0)
    m_b = jnp.where((lag_f <= 0)[..., None, None, None], kern[jnp.clip(-lag_f, 0, tt - 1), 1], 0.0)
    toep = (m_f + m_b).transpose(2, 0, 4, 1, 3)
    g_n, p_n, c_n = b_mat.shape
    toep = toep.reshape(g_n, tt * c_n, tt * c_n)
    pw_f = powers[tt - 1 - jnp.arange(tt), 0]
    pw_b = powers[jnp.arange(tt), 1]
    bin_f = pw_f[..., None] * b_bar[0][None]
    bin_b = pw_b[..., None] * b_bar[1][None]

    def lay_in(bc):
        bc = bc.transpose(1, 0, 3, 2).reshape(g_n, tt * c_n, p_n)
        re, im = jnp.real(bc), jnp.imag(bc)
        return jnp.concatenate([re, im, im, re], axis=-1)

    b_in = jnp.concatenate([lay_in(bin_f), lay_in(bin_b)], axis=-1)
    d_f = c_mat[None] * powers[1 + jnp.arange(tt), 0][:, :, None, :]
    d_b = c_mat[None] * powers[tt - jnp.arange(tt), 1][:, :, None, :]

    def lay_out(dc):
        dc = dc.transpose(1, 3, 0, 2).reshape(g_n, p_n, tt * c_n)
        return jnp.concatenate([jnp.real(dc), -jnp.imag(dc)], axis=1)

    c_out = jnp.concatenate([lay_out(d_f), lay_out(d_b)], axis=1)
    lt = powers[tt]
    lr, li = jnp.real(lt), jnp.imag(lt)
    lam_t = jnp.stack(
        [
            jnp.concatenate([lr[0], lr[0]], -1),
            jnp.concatenate([-li[0], li[0]], -1),
            jnp.concatenate([lr[1], lr[1]], -1),
            jnp.concatenate([-li[1], li[1]], -1),
        ],
        axis=1,
    )
    return toep.astype(BF16), b_in.astype(BF16), c_out.astype(BF16), lam_t


def _s5_kernel(bsz, u_ref, toep_ref, bin_ref, cout_ref, lam_ref, s0_ref, y_ref, s1_ref, v_s, st_s):
    u = u_ref[0].astype(BF16)
    v_s[...] = jnp.dot(u, bin_ref[0], preferred_element_type=F32)
    nk = u.shape[0] // bsz
    p2 = lam_ref.shape[2]
    lam = lam_ref[0]
    a_f, b_f, a_b, b_b = lam[0:1], lam[1:2], lam[2:3], lam[3:4]
    s0 = s0_ref[0]

    def body(k, carry):
        sf, sfx, sb, sbx = carry
        rf = pl.multiple_of(k * bsz, bsz)
        rb = pl.multiple_of((nk - 1 - k) * bsz, bsz)
        st_s[pl.ds(rf, bsz), 0:p2] = sf
        st_s[pl.ds(rb, bsz), p2 : 2 * p2] = sb
        vf = v_s[pl.ds(rf, bsz), 0 : 2 * p2]
        vb = v_s[pl.ds(rb, bsz), 2 * p2 : 4 * p2]
        sf_n = a_f * sf + b_f * sfx + vf[:, 0:p2]
        sfx_n = a_f * sfx - b_f * sf + vf[:, p2 : 2 * p2]
        sb_n = a_b * sb + b_b * sbx + vb[:, 0:p2]
        sbx_n = a_b * sbx - b_b * sb + vb[:, p2 : 2 * p2]
        return sf_n, sfx_n, sb_n, sbx_n

    carry = (s0[:, 0:p2], s0[:, p2 : 2 * p2], s0[:, 2 * p2 : 3 * p2], s0[:, 3 * p2 : 4 * p2])
    sf, sfx, sb, sbx = lax.fori_loop(0, nk, body, carry)
    s1_ref[0] = jnp.concatenate([sf, sfx, sb, sbx], axis=-1)
    y = jnp.dot(u, toep_ref[0], preferred_element_type=F32)
    y += jnp.dot(st_s[...].astype(BF16), cout_ref[0], preferred_element_type=F32)
    y_ref[0] = y


def s5_bidir(u_rows, mats, s0):
    toep, b_in, c_out, lam_t = mats
    g_n, rows, width = u_rows.shape
    bsz = s0.shape[1]
    p8 = b_in.shape[2]
    p2 = p8 // 4
    return pl.pallas_call(
        functools.partial(_s5_kernel, bsz),
        grid=(g_n,),
        in_specs=[
            pl.BlockSpec((1, rows, width), lambda g: (g, 0, 0)),
            pl.BlockSpec((1, width, width), lambda g: (g, 0, 0)),
            pl.BlockSpec((1, width, p8), lambda g: (g, 0, 0)),
            pl.BlockSpec((1, 2 * p2, width), lambda g: (g, 0, 0)),
            pl.BlockSpec((1, 4, p2), lambda g: (g, 0, 0)),
            pl.BlockSpec((1, bsz, p8), lambda g: (g, 0, 0)),
        ],
        out_specs=[
            pl.BlockSpec((1, rows, width), lambda g: (g, 0, 0)),
            pl.BlockSpec((1, bsz, p8), lambda g: (g, 0, 0)),
        ],
        out_shape=[jax.ShapeDtypeStruct(u_rows.shape, F32), jax.ShapeDtypeStruct(s0.shape, F32)],
        scratch_shapes=[pltpu.VMEM((rows, p8), F32), pltpu.VMEM((rows, 2 * p2), F32)],
        compiler_params=_cparams("arbitrary"),
    )(u_rows, toep, b_in, c_out, lam_t, s0)


def _s5_glu_kernel(ys_ref, u_ref, d_ref, w_ref, o_ref):
    y = ys_ref[0] + d_ref[...] * u_ref[0]
    y = jax.nn.gelu(y)
    gate = jax.nn.sigmoid(jnp.dot(y.astype(BF16), w_ref[...], preferred_element_type=F32))
    o_ref[0] = (y * gate).astype(o_ref.dtype)


def s5_glu(ys, z, u_col_block, d_skip, w_glu):
    bsz, seq, sd = ys.shape
    tl = _tile(seq, 512)
    return pl.pallas_call(
        _s5_glu_kernel,
        grid=(bsz, seq // tl),
        in_specs=[
            pl.BlockSpec((1, tl, sd), lambda b, i: (b, i, 0)),
            pl.BlockSpec((1, tl, sd), lambda b, i: (b, i, u_col_block)),
            pl.BlockSpec((1, sd), lambda b, i: (0, 0)),
            pl.BlockSpec((sd, sd), lambda b, i: (0, 0)),
        ],
        out_specs=pl.BlockSpec((1, tl, sd), lambda b, i: (b, i, 0)),
        out_shape=jax.ShapeDtypeStruct((bsz, seq, sd), BF16),
        compiler_params=_cparams("arbitrary", "arbitrary"),
    )(ys, z, d_skip.reshape(1, sd), w_glu)


def _to_s5_rows(z, u_col0, sd):
    bsz, seq = z.shape[:2]
    g_n = sd // S5_GROUP
    u = z[:, :, u_col0 : u_col0 + sd].astype(BF16).reshape(bsz, seq // S5_TILE, S5_TILE, g_n, S5_GROUP)
    return u.transpose(3, 1, 0, 2, 4).reshape(g_n, (seq // S5_TILE) * bsz, S5_TILE * S5_GROUP)


def _from_s5_rows(y, bsz):
    g_n, rows, width = y.shape
    nk = rows // bsz
    y = y.reshape(g_n, nk, bsz, S5_TILE, S5_GROUP).transpose(2, 1, 3, 0, 4)
    return y.reshape(bsz, nk * S5_TILE, g_n * S5_GROUP)


def _hybrid_prep(h, w_main, w_gates, conv_w, col_scale, md):
    bsz, seq, d = h.shape
    z = matmul([(h.reshape(bsz * seq, d), w_main)], F32).reshape(bsz, seq, -1)
    gates = matmul([(h.reshape(bsz * seq, d), w_gates)], F32, tn=LANES).reshape(bsz, seq, LANES)
    qk = qk_conv(z, conv_w, col_scale, 2 * md)
    return z, gates, qk


def hybrid_mixer(h_c, h_l, p, need_ctx):
    d = h_l.shape[2]
    md = p["head_norm"].shape[0]
    sd = d - md
    nh = MLSTM_HEADS
    dh = md // nh
    bsz = h_l.shape[0]
    g_n = sd // S5_GROUP
    v_blk, o_blk, u_blk = 2, 3, 4
    outs = []
    m_state = (
        jnp.zeros((bsz, 2 * nh, dh, dh), F32),
        jnp.zeros((bsz, 2 * nh, 8, dh), F32),
        jnp.zeros((bsz, 2 * nh, 8, LANES), F32),
    )
    s_state = jnp.zeros((g_n, bsz, 8 * p["s5_mats"][3].shape[2] // 2), F32)
    for h, want_out in ((h_c, need_ctx), (h_l, True)):
        z, gates, qk = _hybrid_prep(h, p["w_main"], p["w_gates"], p["conv"], p["col_scale"], md)
        seq = h.shape[1]
        h_f, h_b, m_state = mlstm_bidir(
            qk, jnp.swapaxes(qk, 1, 2), z, v_blk, gates, jnp.swapaxes(gates[:, :, : 4 * nh], 1, 2), p["b_gates"], m_state
        )
        ys_rows, s_state = s5_bidir(_to_s5_rows(z, u_blk * sd, sd), p["s5_mats"], s_state)
        if not want_out:
            outs.append(None)
            continue
        hm = mlstm_finish(h_f, h_b, z, o_blk, p["head_norm"])
        y = s5_glu(_from_s5_rows(ys_rows, bsz), z, u_blk, p["d_skip"], p["w_glu"])
        outs.append((hm.reshape(bsz * seq, md), y.reshape(bsz * seq, sd)))
    return outs


def kernel(x, c, ctx, c_ctx, ada_w, ada_b, norm_mix, norm_ffn, norm_final, hy_w_in, hy_b_gates, hy_conv, hy_head_norm, s5_lam_re, s5_lam_im, s5_log_step, s5_b_re, s5_b_im, s5_c_re, s5_c_im, s5_d, s5_w_glu, hy_w_out, da_w_qkv, da_lambda, da_head_norm, da_w_o, ec_router, ec_w_gate, ec_w_up, ec_w_down):
    bsz, seq, d = x.shape
    lc = ctx.shape[1]
    depth = ada_w.shape[0]
    md = hy_head_norm.shape[1]
    sd = d - md
    nh = MLSTM_HEADS
    hd = d // (2 * DIFF_HEADS)

    rows = -(-(bsz + 1) // 8) * 8
    cond = jnp.zeros((rows, d), F32).at[:bsz].set(c).at[bsz].set(c_ctx)
    mod = ada_modulation(cond, ada_w, ada_b)
    mod_l = mod[:, :bsz].reshape(depth, bsz, ADA_CHUNKS, 1, d)
    mod_c = jnp.broadcast_to(mod[:, bsz : bsz + 1].reshape(depth, 1, ADA_CHUNKS, 1, d), (depth, bsz, ADA_CHUNKS, 1, d))

    n_rows = seq // GRID_W
    row_pos = jnp.repeat(jnp.arange(n_rows, dtype=F32), GRID_W)
    col_pos = jnp.tile(jnp.arange(GRID_W, dtype=F32), n_rows)
    axis_dim = hd // 2
    inv_freq = ROPE_BASE ** (-jnp.arange(0, axis_dim, 2, dtype=F32) / axis_dim)
    ar = row_pos[:, None] * inv_freq[None]
    ac = col_pos[:, None] * inv_freq[None]
    ang = jnp.concatenate([ar, ar, ac, ac], axis=-1)
    quarter = hd // 4
    sign = jnp.where((jnp.arange(hd) % (2 * quarter)) < quarter, -1.0, 1.0)
    cos_t, sin_t = jnp.cos(ang), jnp.sin(ang) * sign

    for layer in range(depth):
        need_ctx = layer < depth - 1
        j = layer // 2
        ml = [mod_l[layer, :, i] for i in range(ADA_CHUNKS)]
        mc = [mod_c[layer, :, i] for i in range(ADA_CHUNKS)]
        (h_l,) = norm_modulate(x, norm_mix[layer], ml[1], ml[0], [BF16])
        (h_c,) = norm_modulate(ctx, norm_mix[layer], mc[1], mc[0], [BF16])
        if layer % 2 == 0:
            w_in = hy_w_in[j]
            g0 = 4 * md
            g1 = g0 + 4 * nh
            w_main = jnp.concatenate([w_in[:, :g0], w_in[:, g1:]], axis=1).astype(BF16)
            w_gates = jnp.zeros((d, LANES), F32).at[:, : 4 * nh].set(w_in[:, g0:g1]).astype(BF16)
            col_scale = jnp.concatenate([jnp.ones((md,), F32), jnp.full((md,), (md // nh) ** -0.5, F32)]).reshape(1, 2 * md)
            p = dict(
                w_main=w_main, w_gates=w_gates, conv=hy_conv[j], col_scale=col_scale, b_gates=hy_b_gates[j],
                head_norm=hy_head_norm[j], d_skip=s5_d[j], w_glu=s5_w_glu[j].astype(BF16),
                s5_mats=s5_matrices(s5_lam_re[j], s5_lam_im[j], s5_log_step[j], s5_b_re[j], s5_b_im[j], s5_c_re[j], s5_c_im[j]),
            )
            mix_c, mix_l = hybrid_mixer(h_c, h_l, p, need_ctx)
            w_out = hy_w_out[j].astype(BF16)
            pairs_l = [(mix_l[0], w_out[:md]), (mix_l[1], w_out[md:])]
            pairs_c = [(mix_c[0], w_out[:md]), (mix_c[1], w_out[md:])] if need_ctx else None
        else:
            lam_init = 0.8 - 0.6 * math.exp(-0.3 * layer)
            lv = da_lambda[j].astype(F32)
            lam = jnp.exp(jnp.sum(lv[0] * lv[1])) - jnp.exp(jnp.sum(lv[2] * lv[3])) + lam_init
            w_qkv = da_w_qkv[j].astype(BF16)
            q_scale = jnp.concatenate([jnp.full((d,), hd**-0.5 * math.log2(math.e), F32), jnp.ones((2 * d,), F32)]).reshape(1, 3 * d)
            qkv_l = matmul([(h_l.reshape(bsz * seq, d), w_qkv)], BF16, rope=(cos_t, sin_t, 2 * d, seq), col_scale=q_scale).reshape(bsz, seq, 3 * d)
            qkv_c = matmul([(h_c.reshape(bsz * lc, d), w_qkv)], BF16, col_scale=q_scale).reshape(bsz, lc, 3 * d)
            o_l = diff_attention(qkv_l, [qkv_c, qkv_l], lam, da_head_norm[j], 1.0 - lam_init)
            w_o = da_w_o[j].astype(BF16)
            pairs_l = [(o_l.reshape(bsz * seq, d), w_o)]
            pairs_c = None
            if need_ctx:
                o_c = diff_attention(qkv_c, [qkv_c], lam, da_head_norm[j], 1.0 - lam_init)
                pairs_c = [(o_c.reshape(bsz * lc, d), w_o)]
        x = matmul(pairs_l, F32, tm=RESIDUAL_ROW_TILE, tn=d, residual=x.reshape(bsz * seq, d), gate=ml[2], rows_per_batch=seq).reshape(bsz, seq, d)
        w_rt = jnp.zeros((d, LANES), F32).at[:, : ec_router.shape[2]].set(ec_router[layer]).astype(BF16)
        w_g, w_u, w_d = (w[layer].astype(BF16) for w in (ec_w_gate, ec_w_up, ec_w_down))
        x = expert_choice_ffn(x, norm_ffn[layer], ml[4], ml[3], ml[5], w_rt, w_g, w_u, w_d)
        if need_ctx:
            ctx = matmul(pairs_c, F32, tm=RESIDUAL_ROW_TILE, tn=d, residual=ctx.reshape(bsz * lc, d), gate=mc[2], rows_per_batch=lc).reshape(bsz, lc, d)
            ctx = expert_choice_ffn(ctx, norm_ffn[layer], mc[4], mc[3], mc[5], w_rt, w_g, w_u, w_d)
    return final_norm(x, norm_final)
```

```python
import functools
import math

import jax
import jax.numpy as jnp
from jax import lax
from jax.experimental import pallas as pl
from jax.experimental.pallas import tpu as pltpu

F32 = jnp.float32
BF16 = jnp.bfloat16

GRID_W = 64
EPS = 1e-6
ADA_CHUNKS = 6
MLSTM_HEADS = 4
QK_CONV = 3
S5_GROUP = 16
DIFF_HEADS = 8
ROPE_BASE = 10000.0
CAPACITY_FACTOR = 2

MLSTM_TILE = 256
S5_TILE = 16
VMEM_LIMIT_BYTES = 56 * 1024 * 1024
ROW_TILE = 1024
COL_TILE = 1024
RESIDUAL_ROW_TILE = 512
ATTN_Q_TILE = 1024
ATTN_SUB_TILE = 256
ATTN_SCORE_SLOTS = 3
ATTN_KEY_CHUNK = 512
LANES = 128
SUBLANES = 8


def _cparams(*sem):
    return pltpu.CompilerParams(dimension_semantics=sem, vmem_limit_bytes=VMEM_LIMIT_BYTES)


def _tile(n, pref):
    t = min(n, pref)
    while n % t:
        t //= 2
    return t


def _ada_kernel(c_ref, w_ref, b_ref, o_ref):
    cv = c_ref[...]
    a = (cv * jax.nn.sigmoid(cv)).astype(BF16)
    o_ref[0] = jnp.dot(a, w_ref[0].astype(BF16), preferred_element_type=F32) + b_ref[0]


def ada_modulation(cond, ada_w, ada_b):
    depth, d, n = ada_w.shape
    rows = cond.shape[0]
    tn = _tile(n, COL_TILE)
    return pl.pallas_call(
        _ada_kernel,
        grid=(depth, n // tn),
        in_specs=[
            pl.BlockSpec((rows, d), lambda l, j: (0, 0)),
            pl.BlockSpec((1, d, tn), lambda l, j: (l, 0, j)),
            pl.BlockSpec((1, 1, tn), lambda l, j: (l, 0, j)),
        ],
        out_specs=pl.BlockSpec((1, rows, tn), lambda l, j: (l, 0, j)),
        out_shape=jax.ShapeDtypeStruct((depth, rows, n), F32),
        compiler_params=_cparams("arbitrary", "arbitrary"),
    )(cond, ada_w, ada_b.reshape(depth, 1, n))


def _normmod_kernel(x_ref, g_ref, sc_ref, sh_ref, *o_refs):
    x = x_ref[0]
    y = x * lax.rsqrt(jnp.mean(x * x, axis=-1, keepdims=True) + EPS) * g_ref[...]
    y = y * (1.0 + sc_ref[0]) + sh_ref[0]
    for o_ref in o_refs:
        o_ref[0] = y.astype(o_ref.dtype)


def norm_modulate(x, g, scale, shift, dtypes):
    bsz, seq, d = x.shape
    tl = _tile(seq, 512)
    return pl.pallas_call(
        _normmod_kernel,
        grid=(bsz, seq // tl),
        in_specs=[
            pl.BlockSpec((1, tl, d), lambda b, i: (b, i, 0)),
            pl.BlockSpec((1, d), lambda b, i: (0, 0)),
            pl.BlockSpec((1, 1, d), lambda b, i: (b, 0, 0)),
            pl.BlockSpec((1, 1, d), lambda b, i: (b, 0, 0)),
        ],
        out_specs=[pl.BlockSpec((1, tl, d), lambda b, i: (b, i, 0)) for _ in dtypes],
        out_shape=[jax.ShapeDtypeStruct(x.shape, dt) for dt in dtypes],
        compiler_params=_cparams("arbitrary", "arbitrary"),
    )(x, g.reshape(1, d), scale, shift)


def _final_norm_kernel(x_ref, g_ref, o_ref):
    x = x_ref[0]
    o_ref[0] = x * lax.rsqrt(jnp.mean(x * x, axis=-1, keepdims=True) + EPS) * g_ref[...]


def final_norm(x, g):
    bsz, seq, d = x.shape
    tl = _tile(seq, 512)
    return pl.pallas_call(
        _final_norm_kernel,
        grid=(bsz, seq // tl),
        in_specs=[pl.BlockSpec((1, tl, d), lambda b, i: (b, i, 0)), pl.BlockSpec((1, d), lambda b, i: (0, 0))],
        out_specs=pl.BlockSpec((1, tl, d), lambda b, i: (b, i, 0)),
        out_shape=jax.ShapeDtypeStruct(x.shape, F32),
        compiler_params=_cparams("arbitrary", "arbitrary"),
    )(x, g.reshape(1, d))


def _mm_kernel(n_pairs, *refs):
    o_ref = refs[2 * n_pairs]
    acc = jnp.dot(refs[0][...], refs[1][...], preferred_element_type=F32)
    for p in range(1, n_pairs):
        acc += jnp.dot(refs[2 * p][...], refs[2 * p + 1][...], preferred_element_type=F32)
    o_ref[...] = acc.astype(o_ref.dtype)


def _mm_colscale_kernel(a_ref, w_ref, cs_ref, o_ref):
    acc = jnp.dot(a_ref[...], w_ref[...], preferred_element_type=F32)
    o_ref[...] = (acc * cs_ref[...]).astype(o_ref.dtype)


def _mm_residual_kernel(n_pairs, *refs):
    r_ref, g_ref, o_ref = refs[2 * n_pairs : 2 * n_pairs + 3]
    acc = jnp.dot(refs[0][...], refs[1][...], preferred_element_type=F32)
    for p in range(1, n_pairs):
        acc += jnp.dot(refs[2 * p][...], refs[2 * p + 1][...], preferred_element_type=F32)
    o_ref[...] = r_ref[...] + g_ref[0] * acc


def _mm_rope_kernel(quarter, a_ref, w_ref, cs_ref, cos_ref, sin_ref, o_ref):
    acc = jnp.dot(a_ref[...], w_ref[...], preferred_element_type=F32) * cs_ref[...]
    cos = cos_ref[...]
    sin = sin_ref[...]
    hd = cos.shape[1]
    lane = lax.broadcasted_iota(jnp.int32, (acc.shape[0], hd), 1)
    first = (lane % (2 * quarter)) < quarter
    for g in range(acc.shape[1] // hd):
        xg = acc[:, g * hd : (g + 1) * hd]
        up = pltpu.roll(xg, hd - quarter, 1)
        dn = pltpu.roll(xg, quarter, 1)
        o_ref[:, g * hd : (g + 1) * hd] = (xg * cos + jnp.where(first, up, dn) * sin).astype(o_ref.dtype)


def matmul(pairs, out_dtype, *, tm=ROW_TILE, tn=COL_TILE, residual=None, gate=None, rows_per_batch=None, rope=None,
           col_scale=None, in_place=True):
    m = pairs[0][0].shape[0]
    n = pairs[0][1].shape[1]
    if rope is not None:
        rows_per_batch = rope[3]
    tm = _tile(m if rows_per_batch is None else rows_per_batch, tm)
    tn = _tile(n, tn)
    grid = (m // tm, n // tn)
    in_specs, args = [], []
    for a, w in pairs:
        k = a.shape[1]
        in_specs += [pl.BlockSpec((tm, k), lambda i, j: (i, 0)), pl.BlockSpec((k, tn), lambda i, j: (0, j))]
        args += [a, w]
    out_spec = pl.BlockSpec((tm, tn), lambda i, j: (i, j))
    aliases = {}
    if residual is not None:
        bpt = rows_per_batch // tm
        in_specs += [pl.BlockSpec((tm, tn), lambda i, j: (i, j)), pl.BlockSpec((1, 1, tn), lambda i, j: (i // bpt, 0, j))]
        args += [residual, gate]
        aliases = {2 * len(pairs): 0} if in_place else {}
        body = functools.partial(_mm_residual_kernel, len(pairs))
    elif rope is not None:
        cos, sin, n_rope_cols, rpb = rope
        hd = cos.shape[1]
        bpt = rpb // tm
        n_rope_tiles = n_rope_cols // tn
        cos = jnp.concatenate([cos, jnp.ones_like(cos)], axis=0)
        sin = jnp.concatenate([sin, jnp.zeros_like(sin)], axis=0)
        table_rows = lambda i, j: (i % bpt + jnp.where(j < n_rope_tiles, 0, bpt), 0)
        in_specs += [
            pl.BlockSpec((1, tn), lambda i, j: (0, j)),
            pl.BlockSpec((tm, hd), table_rows),
            pl.BlockSpec((tm, hd), table_rows),
        ]
        args += [col_scale, cos, sin]
        body = functools.partial(_mm_rope_kernel, hd // 4)
    elif col_scale is not None:
        in_specs.append(pl.BlockSpec((1, tn), lambda i, j: (0, j)))
        args.append(col_scale)
        body = _mm_colscale_kernel
    else:
        body = functools.partial(_mm_kernel, len(pairs))
    return pl.pallas_call(
        body,
        grid=grid,
        in_specs=in_specs,
        out_specs=out_spec,
        out_shape=jax.ShapeDtypeStruct((m, n), out_dtype),
        input_output_aliases=aliases,
        compiler_params=_cparams("arbitrary", "arbitrary"),
    )(*args)


def _attn_kernel(n_seg, seg_off, hd, sub, out_scale, lam_ref, q_ref, *refs):
    kv_refs = refs[: 2 * n_seg]
    g_ref, o_ref, s_scr, p_scr = refs[2 * n_seg : 2 * n_seg + 4]
    lam = lam_ref[0]
    nt = (((1,), (1,)), ((), ()))
    n_sub = q_ref.shape[1] // sub
    n_chain = 2 * n_sub
    n_s, n_p = s_scr.shape[0], p_scr.shape[0]
    row_max, row_sum, outs = {}, {}, {}
    max_w = math.gcd(LANES, *[seg_off[s + 1] - seg_off[s] for s in range(n_seg)])

    def stage_scores(c):
        r, i = divmod(c, 2)
        qi = q_ref[0, r * sub : (r + 1) * sub, i * hd : (i + 1) * hd]
        m_lanes = None
        for s in range(n_seg):
            seg_len = seg_off[s + 1] - seg_off[s]
            step = min(ATTN_KEY_CHUNK, seg_len)
            for k0 in range(0, seg_len, step):
                kc = kv_refs[2 * s][0, k0 : k0 + step, i * hd : (i + 1) * hd]
                sc = lax.dot_general(qi, kc, nt, preferred_element_type=F32)
                s_scr[c % n_s, :, seg_off[s] + k0 : seg_off[s] + k0 + step] = sc
                for l0 in range(0, step, max_w):
                    piece = sc[:, l0 : l0 + max_w]
                    m_lanes = piece if m_lanes is None else jnp.maximum(m_lanes, piece)
        row_max[c] = m_lanes.max(axis=-1, keepdims=True)

    def stage_probs(c):
        p = jnp.exp2(s_scr[c % n_s] - row_max[c])
        row_sum[c] = p.sum(axis=-1, keepdims=True)
        p_scr[c % n_p] = p.astype(BF16)

    def stage_values(c):
        acc = None
        for s in range(n_seg):
            lo, hi = seg_off[s], seg_off[s + 1]
            part = jnp.dot(p_scr[c % n_p, :, lo:hi], kv_refs[2 * s + 1][0], preferred_element_type=F32)
            acc = part if acc is None else acc + part
        outs[c] = acc / row_sum[c]
        if c % 2 == 1:
            r = c // 2
            o = outs.pop(c - 1) - lam * outs.pop(c)
            o = o * lax.rsqrt(jnp.mean(o * o, axis=-1, keepdims=True) + EPS) * g_ref[...] * out_scale
            o_ref[0, r * sub : (r + 1) * sub, :] = o.astype(o_ref.dtype)

    for t in range(n_chain + 2):
        if t < n_chain:
            stage_scores(t)
        if 0 <= t - 1 < n_chain:
            stage_probs(t - 1)
        if 0 <= t - 2 < n_chain:
            stage_values(t - 2)


def diff_attention(q_src, kv_srcs, lam, head_norm, out_scale):
    bsz, lq, d3 = q_src.shape
    d = d3 // 3
    hw = d // DIFF_HEADS
    hd = hw // 2
    tq = _tile(lq, ATTN_Q_TILE)
    sub = _tile(tq, ATTN_SUB_TILE)
    seg_off = [0]
    for src in kv_srcs:
        seg_off.append(seg_off[-1] + src.shape[1])
    in_specs = [
        pl.BlockSpec(memory_space=pltpu.SMEM),
        pl.BlockSpec((1, tq, hw), lambda b, h, i: (b, i, h)),
    ]
    args = [lam.reshape(1), q_src]
    for src in kv_srcs:
        ls = src.shape[1]
        in_specs += [
            pl.BlockSpec((1, ls, hw), lambda b, h, i: (b, 0, DIFF_HEADS + h)),
            pl.BlockSpec((1, ls, hw), lambda b, h, i: (b, 0, 2 * DIFF_HEADS + h)),
        ]
        args += [src, src]
    in_specs.append(pl.BlockSpec((1, hw), lambda b, h, i: (0, h)))
    args.append(head_norm.reshape(1, d))
    return pl.pallas_call(
        functools.partial(_attn_kernel, len(kv_srcs), tuple(seg_off), hd, sub, out_scale),
        grid=(bsz, DIFF_HEADS, lq // tq),
        in_specs=in_specs,
        out_specs=pl.BlockSpec((1, tq, hw), lambda b, h, i: (b, i, h)),
        out_shape=jax.ShapeDtypeStruct((bsz, lq, d), BF16),
        scratch_shapes=[
            pltpu.VMEM((ATTN_SCORE_SLOTS, sub, seg_off[-1]), F32),
            pltpu.VMEM((ATTN_SCORE_SLOTS, sub, seg_off[-1]), BF16),
        ],
        compiler_params=_cparams("arbitrary", "arbitrary", "arbitrary"),
    )(*args)


def _rms_modulate(x, g, scale, shift):
    y = x * lax.rsqrt(jnp.mean(x * x, axis=-1, keepdims=True) + EPS) * g
    return y * (1.0 + scale) + shift


def _normmod_router_kernel(x_ref, g_ref, sc_ref, sh_ref, w_ref, h_ref, aff_ref):
    y = _rms_modulate(x_ref[0], g_ref[...], sc_ref[0], sh_ref[0])
    h_ref[0] = y
    logits = jnp.dot(y.astype(BF16), w_ref[...], preferred_element_type=F32).T[: aff_ref.shape[1]]
    z = jnp.exp(logits - logits.max(axis=0, keepdims=True))
    aff_ref[0] = z / z.sum(axis=0, keepdims=True)


def norm_modulate_router(x, g, scale, shift, w_router, ne):
    bsz, seq, d = x.shape
    tl = _tile(seq, 512)
    return pl.pallas_call(
        _normmod_router_kernel,
        grid=(bsz, seq // tl),
        in_specs=[
            pl.BlockSpec((1, tl, d), lambda b, i: (b, i, 0)),
            pl.BlockSpec((1, d), lambda b, i: (0, 0)),
            pl.BlockSpec((1, 1, d), lambda b, i: (b, 0, 0)),
            pl.BlockSpec((1, 1, d), lambda b, i: (b, 0, 0)),
            pl.BlockSpec((d, LANES), lambda b, i: (0, 0)),
        ],
        out_specs=[pl.BlockSpec((1, tl, d), lambda b, i: (b, i, 0)), pl.BlockSpec((1, ne, tl), lambda b, i: (b, 0, i))],
        out_shape=[jax.ShapeDtypeStruct(x.shape, F32), jax.ShapeDtypeStruct((bsz, ne, seq), F32)],
        compiler_params=_cparams("arbitrary", "arbitrary"),
    )(x, g.reshape(1, d), scale, shift, w_router)


def _moe_kernel(cap, nb, idx_ref, idx_next_ref, gate_ref, mod_ref, h_hbm, x_in, wg_ref, wu_ref, wd_ref, x_out,
                xs_buf, xr_buf, sems):
    del x_in
    e, b = pl.program_id(0), pl.program_id(1)
    n = e * nb + b
    n_last = pl.num_programs(0) * nb - 1
    slot = n % 2
    other = 1 - slot
    sem_h, sem_in, sem_out = 0, 1, 2

    def start_rows(rows_of, idx, buf_slot, sem, to_hbm=False):
        def body(g, carry):
            for k in range(SUBLANES):
                hbm_row = rows_of.at[pl.ds(idx[0, 0, g * SUBLANES + k], 1)]
                buf_row = buf_slot.at[g, pl.ds(k, 1)]
                if to_hbm:
                    pltpu.make_async_copy(buf_row, hbm_row, sem).start()
                else:
                    pltpu.make_async_copy(hbm_row, buf_row, sem).start()
            return carry

        lax.fori_loop(0, cap // SUBLANES, body, 0)

    def wait_all(buf_slot, sem):
        pltpu.make_async_copy(buf_slot, buf_slot, sem).wait()

    @pl.when(n == 0)
    def _():
        start_rows(h_hbm.at[b], idx_ref, xs_buf.at[slot], sems.at[sem_h, slot])

    @pl.when(n < n_last)
    def _():
        start_rows(h_hbm.at[(b + 1) % nb], idx_next_ref, xs_buf.at[other], sems.at[sem_h, other])

    start_rows(x_out.at[b], idx_ref, xr_buf.at[slot], sems.at[sem_in, slot])

    wait_all(xs_buf.at[slot], sems.at[sem_h, slot])
    d = xs_buf.shape[-1]
    xs = xs_buf[slot].reshape(cap, d).astype(BF16)
    gt = jnp.dot(xs, wg_ref[0], preferred_element_type=F32)
    up = jnp.dot(xs, wu_ref[0], preferred_element_type=F32)
    hid = (gt * jax.nn.sigmoid(gt) * up).astype(BF16)
    out = jnp.dot(hid, wd_ref[0], preferred_element_type=F32) * gate_ref[0] * mod_ref[0]
    wait_all(xr_buf.at[slot], sems.at[sem_in, slot])
    xr_buf[slot] += out.reshape(cap // SUBLANES, SUBLANES, d)

    @pl.when(n > 0)
    def _():
        wait_all(xr_buf.at[other], sems.at[sem_out, other])

    start_rows(x_out.at[b], idx_ref, xr_buf.at[slot], sems.at[sem_out, slot], to_hbm=True)

    @pl.when(n == n_last)
    def _():
        wait_all(xr_buf.at[slot], sems.at[sem_out, slot])


def moe_apply(x, h, idx, gate, mod, w_gate, w_up, w_down):
    bsz, seq, d = x.shape
    ne, cap = idx.shape[1], idx.shape[2]
    ff = w_gate.shape[2]
    idx_rows = idx.reshape(bsz * ne, 1, cap)

    def next_rows(e, b):
        nxt = jnp.minimum(e * bsz + b + 1, ne * bsz - 1)
        return ((nxt % bsz) * ne + nxt // bsz, 0, 0)

    single = pl.Buffered(1)
    return pl.pallas_call(
        functools.partial(_moe_kernel, cap, bsz),
        grid=(ne, bsz),
        in_specs=[
            pl.BlockSpec((1, 1, cap), lambda e, b: (b * ne + e, 0, 0), memory_space=pltpu.SMEM),
            pl.BlockSpec((1, 1, cap), next_rows, memory_space=pltpu.SMEM),
            pl.BlockSpec((1, cap, 1), lambda e, b: (b * ne + e, 0, 0)),
            pl.BlockSpec((1, 1, d), lambda e, b: (b, 0, 0)),
            pl.BlockSpec(memory_space=pl.ANY),
            pl.BlockSpec(memory_space=pl.ANY),
            pl.BlockSpec((1, d, ff), lambda e, b: (e, 0, 0), pipeline_mode=single),
            pl.BlockSpec((1, d, ff), lambda e, b: (e, 0, 0), pipeline_mode=single),
            pl.BlockSpec((1, ff, d), lambda e, b: (e, 0, 0), pipeline_mode=single),
        ],
        out_specs=pl.BlockSpec(memory_space=pl.ANY),
        out_shape=jax.ShapeDtypeStruct(x.shape, F32),
        scratch_shapes=[
            pltpu.VMEM((2, cap // SUBLANES, SUBLANES, d), F32),
            pltpu.VMEM((2, cap // SUBLANES, SUBLANES, d), F32),
            pltpu.SemaphoreType.DMA((3, 2)),
        ],
        input_output_aliases={5: 0},
        compiler_params=_cparams("arbitrary", "arbitrary"),
    )(idx_rows, idx_rows, gate.reshape(bsz * ne, cap, 1), mod, h, x, w_gate, w_up, w_down)


def expert_choice_ffn(x, norm_g, scale, shift, mod, w_router, w_gate, w_up, w_down):
    seq = x.shape[1]
    ne = w_gate.shape[0]
    cap = CAPACITY_FACTOR * seq // ne
    h, aff = norm_modulate_router(x, norm_g, scale, shift, w_router, ne)
    gate, idx = lax.top_k(aff, cap)
    return moe_apply(x, h, idx.astype(jnp.int32), gate, mod, w_gate, w_up, w_down)


def _conv_kernel(z_ref, w_ref, s_ref, o_ref):
    x = z_ref[0]
    seq = x.shape[0]
    row = lax.broadcasted_iota(jnp.int32, x.shape, 0)
    prev = jnp.where(row == 0, 0.0, pltpu.roll(x, 1, 0))
    nxt = jnp.where(row == seq - 1, 0.0, pltpu.roll(x, seq - 1, 0))
    w = w_ref[...]
    y = prev * w[0:1] + x * w[1:2] + nxt * w[2:3]
    o_ref[0] = (y * jax.nn.sigmoid(y) * s_ref[...]).astype(o_ref.dtype)


def qk_conv(z, conv_w, col_scale, n_cols):
    bsz, seq = z.shape[:2]
    tc = 256
    return pl.pallas_call(
        _conv_kernel,
        grid=(bsz, n_cols // tc),
        in_specs=[
            pl.BlockSpec((1, seq, tc), lambda b, j: (b, 0, j)),
            pl.BlockSpec((QK_CONV, tc), lambda b, j: (0, j)),
            pl.BlockSpec((1, tc), lambda b, j: (0, j)),
        ],
        out_specs=pl.BlockSpec((1, seq, tc), lambda b, j: (b, 0, j)),
        out_shape=jax.ShapeDtypeStruct((bsz, seq, n_cols), BF16),
        compiler_params=_cparams("arbitrary", "arbitrary"),
    )(z, conv_w, col_scale)


def _log_sigmoid(x):
    return jnp.minimum(x, 0.0) - jnp.log(1.0 + jnp.exp(-jnp.abs(x)))


def _mlstm_kernel(nh, dh, *refs):
    dir_refs = (refs[0:6], refs[6:12])
    bias_ref, bias_t_ref, c0_ref, n0_ref, m0_ref = refs[12:17]
    h_outs = refs[17:19]
    c_out, n_out, m_out = refs[19:22]
    c_s, n_s, m_s = refs[22:25]
    kc = pl.program_id(1)
    last = pl.num_programs(1) - 1

    @pl.when(kc == 0)
    def _():
        c_s[...] = c0_ref[0]
        n_s[...] = n0_ref[0]
        m_s[...] = m0_ref[0]

    tt = dir_refs[0][0].shape[1]
    row = lax.broadcasted_iota(jnp.int32, (tt, tt), 0)
    col = lax.broadcasted_iota(jnp.int32, (tt, tt), 1)
    nt = (((1,), (1,)), ((), ()))
    hi = lax.Precision.HIGHEST
    for d in range(2):
        q_ref, k_ref, kt_ref, v_ref, g_ref, gt_ref = dir_refs[d]
        window = (col <= row) if d == 0 else (col >= row)
        tri = window.astype(F32)
        gates = g_ref[0] + bias_ref[...]
        gates_t = gt_ref[0] + bias_t_ref[...]
        b_cols = jnp.dot(tri, _log_sigmoid(gates), precision=hi, preferred_element_type=F32)
        b_rows = lax.dot_general(_log_sigmoid(gates_t), tri, nt, precision=hi, preferred_element_type=F32)
        edge = tt - 1 if d == 0 else 0
        for h in range(nh):
            ci = 2 * nh * d + h
            cf = ci + nh
            sl = slice(h * dh, (h + 1) * dh)
            st = d * nh + h
            i_col, i_row = gates[:, ci : ci + 1], gates_t[ci : ci + 1, :]
            b_col, b_row = b_cols[:, cf : cf + 1], b_rows[cf : cf + 1, :]
            m_prev = m_s[st][0:1, 0:1]
            log_d = jnp.where(window, b_col - b_row + i_row, -jnp.inf)
            inter = b_col + m_prev
            m_t = jnp.maximum(log_d.max(axis=-1, keepdims=True), inter)
            d_mat = jnp.exp(log_d - m_t)
            w_inter = jnp.exp(inter - m_t)
            qh = q_ref[0][:, sl]
            vh = v_ref[0][:, sl]
            s_mat = jnp.dot(qh, kt_ref[0][sl, :], preferred_element_type=F32) * d_mat
            c_t = c_s[st]
            n_row = n_s[st][0:1, :]
            num = jnp.dot(s_mat.astype(BF16), vh.astype(BF16), preferred_element_type=F32)
            num += w_inter * jnp.dot(qh, c_t.astype(BF16), preferred_element_type=F32)
            den = s_mat.sum(axis=-1, keepdims=True) + w_inter * (qh.astype(F32) * n_row).sum(axis=-1, keepdims=True)
            h_outs[d][0, :, sl] = num / jnp.maximum(jnp.abs(den), jnp.exp(-m_t))
            b_last = b_col[edge : edge + 1, :]
            ws_col = b_last - b_col + i_col
            m_new = jnp.maximum(b_last + m_prev, ws_col.max(axis=0, keepdims=True))
            decay = jnp.exp(b_last + m_prev - m_new)
            ws_col = jnp.exp(ws_col - m_new)
            c_s[st] = decay * c_t + jnp.dot(kt_ref[0][sl, :], (ws_col * vh).astype(BF16), preferred_element_type=F32)
            n_new = decay * n_row + (ws_col * k_ref[0][:, sl].astype(F32)).sum(axis=0, keepdims=True)
            n_s[st] = jnp.broadcast_to(n_new, n_s.shape[1:])
            m_s[st] = jnp.broadcast_to(m_new, m_s.shape[1:])

    @pl.when(kc == last)
    def _():
        c_out[0] = c_s[...]
        n_out[0] = n_s[...]
        m_out[0] = m_s[...]


def mlstm_bidir(qk, qk_t, z, v_col_block, gates, gates_t, bias, state):
    bsz, seq, m2 = qk.shape
    md = m2 // 2
    nh = MLSTM_HEADS
    dh = md // nh
    tt = _tile(seq, MLSTM_TILE)
    nc = seq // tt
    fwd = lambda b, k: k
    bwd = lambda b, k: nc - 1 - k
    in_specs, args = [], []
    for ck in (fwd, bwd):
        in_specs += [
            pl.BlockSpec((1, tt, md), lambda b, k, ck=ck: (b, ck(b, k), 0)),
            pl.BlockSpec((1, tt, md), lambda b, k, ck=ck: (b, ck(b, k), 1)),
            pl.BlockSpec((1, md, tt), lambda b, k, ck=ck: (b, 0, ck(b, k))),
            pl.BlockSpec((1, tt, md), lambda b, k, ck=ck: (b, ck(b, k), v_col_block)),
            pl.BlockSpec((1, tt, LANES), lambda b, k, ck=ck: (b, ck(b, k), 0)),
            pl.BlockSpec((1, 4 * nh, tt), lambda b, k, ck=ck: (b, 0, ck(b, k))),
        ]
        args += [qk, qk, qk_t, z, gates, gates_t]
    c0, n0, m0 = state
    ns = 2 * nh
    in_specs += [
        pl.BlockSpec((1, LANES), lambda b, k: (0, 0)),
        pl.BlockSpec((4 * nh, 1), lambda b, k: (0, 0)),
        pl.BlockSpec((1, ns, dh, dh), lambda b, k: (b, 0, 0, 0)),
        pl.BlockSpec((1, ns, 8, dh), lambda b, k: (b, 0, 0, 0)),
        pl.BlockSpec((1, ns, 8, LANES), lambda b, k: (b, 0, 0, 0)),
    ]
    bias_row = jnp.zeros((1, LANES), F32).at[0, : 4 * nh].set(bias)
    args += [bias_row, bias.reshape(4 * nh, 1), c0, n0, m0]
    out_specs = [
        pl.BlockSpec((1, tt, md), lambda b, k: (b, k, 0)),
        pl.BlockSpec((1, tt, md), lambda b, k: (b, nc - 1 - k, 0)),
        pl.BlockSpec((1, ns, dh, dh), lambda b, k: (b, 0, 0, 0)),
        pl.BlockSpec((1, ns, 8, dh), lambda b, k: (b, 0, 0, 0)),
        pl.BlockSpec((1, ns, 8, LANES), lambda b, k: (b, 0, 0, 0)),
    ]
    out_shape = [
        jax.ShapeDtypeStruct((bsz, seq, md), F32),
        jax.ShapeDtypeStruct((bsz, seq, md), F32),
        jax.ShapeDtypeStruct(c0.shape, F32),
        jax.ShapeDtypeStruct(n0.shape, F32),
        jax.ShapeDtypeStruct(m0.shape, F32),
    ]
    h_f, h_b, c1, n1, m1 = pl.pallas_call(
        functools.partial(_mlstm_kernel, nh, dh),
        grid=(bsz, nc),
        in_specs=in_specs,
        out_specs=out_specs,
        out_shape=out_shape,
        scratch_shapes=[pltpu.VMEM((ns, dh, dh), F32), pltpu.VMEM((ns, 8, dh), F32), pltpu.VMEM((ns, 8, LANES), F32)],
        compiler_params=_cparams("arbitrary", "arbitrary"),
    )(*args)
    return h_f, h_b, (c1, n1, m1)


def _mlstm_finish_kernel(nh, hf_ref, hb_ref, o_ref, g_ref, out_ref):
    hm = hf_ref[0] + hb_ref[0]
    dh = hm.shape[1] // nh
    gate = jax.nn.sigmoid(o_ref[0])
    g = g_ref[...]
    for h in range(nh):
        sl = slice(h * dh, (h + 1) * dh)
        xh = hm[:, sl]
        y = xh * lax.rsqrt(jnp.mean(xh * xh, axis=-1, keepdims=True) + EPS) * g[:, sl]
        out_ref[0, :, sl] = (y * gate[:, sl]).astype(out_ref.dtype)


def mlstm_finish(h_f, h_b, z, o_col_block, head_norm):
    bsz, seq, md = h_f.shape
    tl = _tile(seq, 512)
    return pl.pallas_call(
        functools.partial(_mlstm_finish_kernel, MLSTM_HEADS),
        grid=(bsz, seq // tl),
        in_specs=[
            pl.BlockSpec((1, tl, md), lambda b, i: (b, i, 0)),
            pl.BlockSpec((1, tl, md), lambda b, i: (b, i, 0)),
            pl.BlockSpec((1, tl, md), lambda b, i: (b, i, o_col_block)),
            pl.BlockSpec((1, md), lambda b, i: (0, 0)),
        ],
        out_specs=pl.BlockSpec((1, tl, md), lambda b, i: (b, i, 0)),
        out_shape=jax.ShapeDtypeStruct((bsz, seq, md), BF16),
        compiler_params=_cparams("arbitrary", "arbitrary"),
    )(h_f, h_b, z, head_norm.reshape(1, md))


def s5_matrices(lam_re, lam_im, log_step, b_re, b_im, c_re, c_im):
    tt = S5_TILE
    lam = lax.complex(lam_re.astype(F32), lam_im.astype(F32))
    step = jnp.exp(log_step.astype(F32))[..., None]
    lam_dt = lam * step
    lam_bar = jnp.exp(lam_dt)
    b_mat = lax.complex(b_re.astype(F32), b_im.astype(F32))
    b_bar = ((lam_bar - 1.0) / lam)[..., None] * b_mat[None]
    c_mat = lax.complex(c_re.astype(F32), c_im.astype(F32))
    ks = jnp.arange(tt + 1, dtype=F32)
    powers = jnp.exp(lam_dt[None] * ks[:, None, None, None])
    hp = lax.Precision.HIGHEST
    kern = jnp.real(jnp.einsum("gop,kdgp,dgpc->kdgoc", c_mat, powers[:tt], b_bar, precision=hp))
    s_idx = jnp.arange(tt)[:, None]
    t_idx = jnp.arange(tt)[None, :]
    lag_f = t_idx - s_idx
    m_f = jnp.where((lag_f >= 0)[..., None, None, None], kern[jnp.clip(lag_f, 0, tt - 1), 0], 0.0)
    m_b = jnp.where((lag_f <= 0)[..., None, None, None], kern[jnp.clip(-lag_f, 0, tt - 1), 1], 0.0)
    toep = (m_f + m_b).transpose(2, 0, 4, 1, 3)
    g_n, p_n, c_n = b_mat.shape
    toep = toep.reshape(g_n, tt * c_n, tt * c_n)
    pw_f = powers[tt - 1 - jnp.arange(tt), 0]
    pw_b = powers[jnp.arange(tt), 1]
    bin_f = pw_f[..., None] * b_bar[0][None]
    bin_b = pw_b[..., None] * b_bar[1][None]

    def lay_in(bc):
        bc = bc.transpose(1, 0, 3, 2).reshape(g_n, tt * c_n, p_n)
        re, im = jnp.real(bc), jnp.imag(bc)
        return jnp.concatenate([re, im, im, re], axis=-1)

    b_in = jnp.concatenate([lay_in(bin_f), lay_in(bin_b)], axis=-1)
    d_f = c_mat[None] * powers[1 + jnp.arange(tt), 0][:, :, None, :]
    d_b = c_mat[None] * powers[tt - jnp.arange(tt), 1][:, :, None, :]

    def lay_out(dc):
        dc = dc.transpose(1, 3, 0, 2).reshape(g_n, p_n, tt * c_n)
        return jnp.concatenate([jnp.real(dc), -jnp.imag(dc)], axis=1)

    c_out = jnp.concatenate([lay_out(d_f), lay_out(d_b)], axis=1)
    lt = powers[tt]
    lr, li = jnp.real(lt), jnp.imag(lt)
    lam_t = jnp.stack(
        [
            jnp.concatenate([lr[0], lr[0]], -1),
            jnp.concatenate([-li[0], li[0]], -1),
            jnp.concatenate([lr[1], lr[1]], -1),
            jnp.concatenate([-li[1], li[1]], -1),
        ],
        axis=1,
    )
    return toep.astype(BF16), b_in.astype(BF16), c_out.astype(BF16), lam_t


def _s5_kernel(bsz, u_ref, toep_ref, bin_ref, cout_ref, lam_ref, s0_ref, y_ref, s1_ref, v_s, st_s):
    u = u_ref[0].astype(BF16)
    v_s[...] = jnp.dot(u, bin_ref[0], preferred_element_type=F32)
    nk = u.shape[0] // bsz
    p2 = lam_ref.shape[2]
    lam = lam_ref[0]
    a_f, b_f, a_b, b_b = lam[0:1], lam[1:2], lam[2:3], lam[3:4]
    s0 = s0_ref[0]

    def body(k, carry):
        sf, sfx, sb, sbx = carry
        rf = pl.multiple_of(k * bsz, bsz)
        rb = pl.multiple_of((nk - 1 - k) * bsz, bsz)
        st_s[pl.ds(rf, bsz), 0:p2] = sf
        st_s[pl.ds(rb, bsz), p2 : 2 * p2] = sb
        vf = v_s[pl.ds(rf, bsz), 0 : 2 * p2]
        vb = v_s[pl.ds(rb, bsz), 2 * p2 : 4 * p2]
        sf_n = a_f * sf + b_f * sfx + vf[:, 0:p2]
        sfx_n = a_f * sfx - b_f * sf + vf[:, p2 : 2 * p2]
        sb_n = a_b * sb + b_b * sbx + vb[:, 0:p2]
        sbx_n = a_b * sbx - b_b * sb + vb[:, p2 : 2 * p2]
        return sf_n, sfx_n, sb_n, sbx_n

    carry = (s0[:, 0:p2], s0[:, p2 : 2 * p2], s0[:, 2 * p2 : 3 * p2], s0[:, 3 * p2 : 4 * p2])
    sf, sfx, sb, sbx = lax.fori_loop(0, nk, body, carry)
    s1_ref[0] = jnp.concatenate([sf, sfx, sb, sbx], axis=-1)
    y = jnp.dot(u, toep_ref[0], preferred_element_type=F32)
    y += jnp.dot(st_s[...].astype(BF16), cout_ref[0], preferred_element_type=F32)
    y_ref[0] = y


def s5_bidir(u_rows, mats, s0):
    toep, b_in, c_out, lam_t = mats
    g_n, rows, width = u_rows.shape
    bsz = s0.shape[1]
    p8 = b_in.shape[2]
    p2 = p8 // 4
    return pl.pallas_call(
        functools.partial(_s5_kernel, bsz),
        grid=(g_n,),
        in_specs=[
            pl.BlockSpec((1, rows, width), lambda g: (g, 0, 0)),
            pl.BlockSpec((1, width, width), lambda g: (g, 0, 0)),
            pl.BlockSpec((1, width, p8), lambda g: (g, 0, 0)),
            pl.BlockSpec((1, 2 * p2, width), lambda g: (g, 0, 0)),
            pl.BlockSpec((1, 4, p2), lambda g: (g, 0, 0)),
            pl.BlockSpec((1, bsz, p8), lambda g: (g, 0, 0)),
        ],
        out_specs=[
            pl.BlockSpec((1, rows, width), lambda g: (g, 0, 0)),
            pl.BlockSpec((1, bsz, p8), lambda g: (g, 0, 0)),
        ],
        out_shape=[jax.ShapeDtypeStruct(u_rows.shape, F32), jax.ShapeDtypeStruct(s0.shape, F32)],
        scratch_shapes=[pltpu.VMEM((rows, p8), F32), pltpu.VMEM((rows, 2 * p2), F32)],
        compiler_params=_cparams("arbitrary"),
    )(u_rows, toep, b_in, c_out, lam_t, s0)


def _s5_glu_kernel(ys_ref, u_ref, d_ref, w_ref, o_ref):
    y = ys_ref[0] + d_ref[...] * u_ref[0]
    y = jax.nn.gelu(y)
    gate = jax.nn.sigmoid(jnp.dot(y.astype(BF16), w_ref[...], preferred_element_type=F32))
    o_ref[0] = (y * gate).astype(o_ref.dtype)


def s5_glu(ys, z, u_col_block, d_skip, w_glu):
    bsz, seq, sd = ys.shape
    tl = _tile(seq, 512)
    return pl.pallas_call(
        _s5_glu_kernel,
        grid=(bsz, seq // tl),
        in_specs=[
            pl.BlockSpec((1, tl, sd), lambda b, i: (b, i, 0)),
            pl.BlockSpec((1, tl, sd), lambda b, i: (b, i, u_col_block)),
            pl.BlockSpec((1, sd), lambda b, i: (0, 0)),
            pl.BlockSpec((sd, sd), lambda b, i: (0, 0)),
        ],
        out_specs=pl.BlockSpec((1, tl, sd), lambda b, i: (b, i, 0)),
        out_shape=jax.ShapeDtypeStruct((bsz, seq, sd), BF16),
        compiler_params=_cparams("arbitrary", "arbitrary"),
    )(ys, z, d_skip.reshape(1, sd), w_glu)


def _to_s5_rows(z, u_col0, sd):
    bsz, seq = z.shape[:2]
    g_n = sd // S5_GROUP
    u = z[:, :, u_col0 : u_col0 + sd].astype(BF16).reshape(bsz, seq // S5_TILE, S5_TILE, g_n, S5_GROUP)
    return u.transpose(3, 1, 0, 2, 4).reshape(g_n, (seq // S5_TILE) * bsz, S5_TILE * S5_GROUP)


def _from_s5_rows(y, bsz):
    g_n, rows, width = y.shape
    nk = rows // bsz
    y = y.reshape(g_n, nk, bsz, S5_TILE, S5_GROUP).transpose(2, 1, 3, 0, 4)
    return y.reshape(bsz, nk * S5_TILE, g_n * S5_GROUP)


def _hybrid_prep(h, w_main, w_gates, conv_w, col_scale, md):
    bsz, seq, d = h.shape
    z = matmul([(h.reshape(bsz * seq, d), w_main)], F32).reshape(bsz, seq, -1)
    gates = matmul([(h.reshape(bsz * seq, d), w_gates)], F32, tn=LANES).reshape(bsz, seq, LANES)
    qk = qk_conv(z, conv_w, col_scale, 2 * md)
    return z, gates, qk


def hybrid_mixer(h_c, h_l, p, need_ctx):
    d = h_l.shape[2]
    md = p["head_norm"].shape[0]
    sd = d - md
    nh = MLSTM_HEADS
    dh = md // nh
    bsz = h_l.shape[0]
    g_n = sd // S5_GROUP
    v_blk, o_blk, u_blk = 2, 3, 4
    outs = []
    m_state = (
        jnp.zeros((bsz, 2 * nh, dh, dh), F32),
        jnp.zeros((bsz, 2 * nh, 8, dh), F32),
        jnp.zeros((bsz, 2 * nh, 8, LANES), F32),
    )
    s_state = jnp.zeros((g_n, bsz, 8 * p["s5_mats"][3].shape[2] // 2), F32)
    for h, want_out in ((h_c, need_ctx), (h_l, True)):
        z, gates, qk = _hybrid_prep(h, p["w_main"], p["w_gates"], p["conv"], p["col_scale"], md)
        seq = h.shape[1]
        h_f, h_b, m_state = mlstm_bidir(
            qk, jnp.swapaxes(qk[:, :, md:], 1, 2), z, v_blk, gates, jnp.swapaxes(gates[:, :, : 4 * nh], 1, 2), p["b_gates"], m_state
        )
        ys_rows, s_state = s5_bidir(_to_s5_rows(z, u_blk * sd, sd), p["s5_mats"], s_state)
        if not want_out:
            outs.append(None)
            continue
        hm = mlstm_finish(h_f, h_b, z, o_blk, p["head_norm"])
        y = s5_glu(_from_s5_rows(ys_rows, bsz), z, u_blk, p["d_skip"], p["w_glu"])
        outs.append((hm.reshape(bsz * seq, md), y.reshape(bsz * seq, sd)))
    return outs


def kernel(x, c, ctx, c_ctx, ada_w, ada_b, norm_mix, norm_ffn, norm_final, hy_w_in, hy_b_gates, hy_conv, hy_head_norm, s5_lam_re, s5_lam_im, s5_log_step, s5_b_re, s5_b_im, s5_c_re, s5_c_im, s5_d, s5_w_glu, hy_w_out, da_w_qkv, da_lambda, da_head_norm, da_w_o, ec_router, ec_w_gate, ec_w_up, ec_w_down):
    bsz, seq, d = x.shape
    lc = ctx.shape[1]
    depth = ada_w.shape[0]
    md = hy_head_norm.shape[1]
    sd = d - md
    nh = MLSTM_HEADS
    hd = d // (2 * DIFF_HEADS)

    rows = -(-(bsz + 1) // 8) * 8
    cond = jnp.zeros((rows, d), F32).at[:bsz].set(c).at[bsz].set(c_ctx)
    mod = ada_modulation(cond, ada_w, ada_b)
    mod_l = mod[:, :bsz].reshape(depth, bsz, ADA_CHUNKS, 1, d)
    mod_c = jnp.broadcast_to(mod[:, bsz : bsz + 1].reshape(depth, 1, ADA_CHUNKS, 1, d), (depth, bsz, ADA_CHUNKS, 1, d))

    n_rows = seq // GRID_W
    row_pos = jnp.repeat(jnp.arange(n_rows, dtype=F32), GRID_W)
    col_pos = jnp.tile(jnp.arange(GRID_W, dtype=F32), n_rows)
    axis_dim = hd // 2
    inv_freq = ROPE_BASE ** (-jnp.arange(0, axis_dim, 2, dtype=F32) / axis_dim)
    ar = row_pos[:, None] * inv_freq[None]
    ac = col_pos[:, None] * inv_freq[None]
    ang = jnp.concatenate([ar, ar, ac, ac], axis=-1)
    quarter = hd // 4
    sign = jnp.where((jnp.arange(hd) % (2 * quarter)) < quarter, -1.0, 1.0)
    cos_t, sin_t = jnp.cos(ang), jnp.sin(ang) * sign

    for layer in range(depth):
        need_ctx = layer < depth - 1
        j = layer // 2
        ml = [mod_l[layer, :, i] for i in range(ADA_CHUNKS)]
        mc = [mod_c[layer, :, i] for i in range(ADA_CHUNKS)]
        (h_l,) = norm_modulate(x, norm_mix[layer], ml[1], ml[0], [BF16])
        (h_c,) = norm_modulate(ctx, norm_mix[layer], mc[1], mc[0], [BF16])
        if layer % 2 == 0:
            w_in = hy_w_in[j]
            g0 = 4 * md
            g1 = g0 + 4 * nh
            w_main = jnp.concatenate([w_in[:, :g0], w_in[:, g1:]], axis=1).astype(BF16)
            w_gates = jnp.zeros((d, LANES), F32).at[:, : 4 * nh].set(w_in[:, g0:g1]).astype(BF16)
            col_scale = jnp.concatenate([jnp.ones((md,), F32), jnp.full((md,), (md // nh) ** -0.5, F32)]).reshape(1, 2 * md)
            p = dict(
                w_main=w_main, w_gates=w_gates, conv=hy_conv[j], col_scale=col_scale, b_gates=hy_b_gates[j],
                head_norm=hy_head_norm[j], d_skip=s5_d[j], w_glu=s5_w_glu[j].astype(BF16),
                s5_mats=s5_matrices(s5_lam_re[j], s5_lam_im[j], s5_log_step[j], s5_b_re[j], s5_b_im[j], s5_c_re[j], s5_c_im[j]),
            )
            mix_c, mix_l = hybrid_mixer(h_c, h_l, p, need_ctx)
            w_out = hy_w_out[j].astype(BF16)
            pairs_l = [(mix_l[0], w_out[:md]), (mix_l[1], w_out[md:])]
            pairs_c = [(mix_c[0], w_out[:md]), (mix_c[1], w_out[md:])] if need_ctx else None
        else:
            lam_init = 0.8 - 0.6 * math.exp(-0.3 * layer)
            lv = da_lambda[j].astype(F32)
            lam = jnp.exp(jnp.sum(lv[0] * lv[1])) - jnp.exp(jnp.sum(lv[2] * lv[3])) + lam_init
            w_qkv = da_w_qkv[j].astype(BF16)
            q_scale = jnp.concatenate([jnp.full((d,), hd**-0.5 * math.log2(math.e), F32), jnp.ones((2 * d,), F32)]).reshape(1, 3 * d)
            qkv_l = matmul([(h_l.reshape(bsz * seq, d), w_qkv)], BF16, rope=(cos_t, sin_t, 2 * d, seq), col_scale=q_scale).reshape(bsz, seq, 3 * d)
            qkv_c = matmul([(h_c.reshape(bsz * lc, d), w_qkv)], BF16, col_scale=q_scale).reshape(bsz, lc, 3 * d)
            o_l = diff_attention(qkv_l, [qkv_c, qkv_l], lam, da_head_norm[j], 1.0 - lam_init)
            w_o = da_w_o[j].astype(BF16)
            pairs_l = [(o_l.reshape(bsz * seq, d), w_o)]
            pairs_c = None
            if need_ctx:
                o_c = diff_attention(qkv_c, [qkv_c], lam, da_head_norm[j], 1.0 - lam_init)
                pairs_c = [(o_c.reshape(bsz * lc, d), w_o)]
        own = layer > 0
        x = matmul(pairs_l, F32, tm=RESIDUAL_ROW_TILE, tn=d, residual=x.reshape(bsz * seq, d), gate=ml[2], rows_per_batch=seq, in_place=own).reshape(bsz, seq, d)
        w_rt = jnp.zeros((d, LANES), F32).at[:, : ec_router.shape[2]].set(ec_router[layer]).astype(BF16)
        w_g, w_u, w_d = (w[layer].astype(BF16) for w in (ec_w_gate, ec_w_up, ec_w_down))
        x = expert_choice_ffn(x, norm_ffn[layer], ml[4], ml[3], ml[5], w_rt, w_g, w_u, w_d)
        if need_ctx:
            ctx = matmul(pairs_c, F32, tm=RESIDUAL_ROW_TILE, tn=d, residual=ctx.reshape(bsz * lc, d), gate=mc[2], rows_per_batch=lc, in_place=own).reshape(bsz, lc, d)
            ctx = expert_choice_ffn(ctx, norm_ffn[layer], mc[4], mc[3], mc[5], w_rt, w_g, w_u, w_d)
    return final_norm(x, norm_final)
```
